```python
import jax, jax.numpy as jnp
from jax import lax
import numpy as np

D_MODEL = 1024
BATCH = 8
SEQ = 2048
DEPTH = 1

MIX_WIDTH = D_MODEL
HEAD_DIM = 64
CONV_WIDTH = MIX_WIDTH // 2
CFM_WIDTH = MIX_WIDTH - CONV_WIDTH
N_HEADS = MIX_WIDTH // HEAD_DIM
SHORT_K = 3
CFM_K = 31
IN_COLS = 3 * CONV_WIDTH + 2 * CFM_WIDTH
FFN_HIDDEN = ((8 * D_MODEL // 3 + 255) // 256) * 256
N_MOD = 6
EPS = 1e-6

kernel_name = "hybrid_shortconv_conformer_adaln_block"


def rmsnorm(x, g):
    xf = x.astype(jnp.float32)
    y = xf * lax.rsqrt(jnp.mean(xf * xf, axis=-1, keepdims=True) + EPS)
    return (y * g.astype(jnp.float32)).astype(x.dtype)


def layernorm(x, g, b):
    xf = x.astype(jnp.float32)
    mu = jnp.mean(xf, axis=-1, keepdims=True)
    var = jnp.mean(jnp.square(xf - mu), axis=-1, keepdims=True)
    y = (xf - mu) * lax.rsqrt(var + EPS)
    return (y * g.astype(jnp.float32) + b.astype(jnp.float32)).astype(x.dtype)


def causal_dwconv(u, w, b):
    k = w.shape[0]
    out = lax.conv_general_dilated(
        u, w[:, None, :].astype(u.dtype), window_strides=(1,), padding=[(k - 1, 0)],
        dimension_numbers=("NWC", "WIO", "NWC"), feature_group_count=u.shape[-1])
    return out + b.astype(u.dtype)


def modulate(h, shift, scale):
    return h * (1.0 + scale[:, None, :]) + shift[:, None, :]


def setup_inputs(seed: int = 0) -> dict:
    key = jax.random.key(seed)
    ks = jax.random.split(key, 24)
    f32 = jnp.float32
    nrm = lambda k, shape, s: jax.random.normal(k, shape, f32) * s
    L = DEPTH
    return {
        "x": jax.random.normal(ks[0], (BATCH, SEQ, D_MODEL), f32),
        "c": jax.random.normal(ks[1], (BATCH, D_MODEL), f32),
        "w_ada": nrm(ks[2], (L, D_MODEL, N_MOD * D_MODEL), 0.5 * D_MODEL ** -0.5),
        "b_ada": nrm(ks[3], (L, N_MOD * D_MODEL), 0.02),
        "g_pre_mix": 1.0 + nrm(ks[4], (L, D_MODEL), 0.02),
        "g_post_mix": 1.0 + nrm(ks[5], (L, D_MODEL), 0.02),
        "w_in": nrm(ks[6], (L, D_MODEL, IN_COLS), D_MODEL ** -0.5),
        "w_short": nrm(ks[7], (L, SHORT_K, CONV_WIDTH), SHORT_K ** -0.5),
        "b_short": nrm(ks[8], (L, CONV_WIDTH), 0.02),
        "w_cfm_dw": nrm(ks[9], (L, CFM_K, CFM_WIDTH), CFM_K ** -0.5),
        "b_cfm_dw": nrm(ks[10], (L, CFM_WIDTH), 0.02),
        "g_cfm_ln": 1.0 + nrm(ks[11], (L, CFM_WIDTH), 0.02),
        "b_cfm_ln": nrm(ks[12], (L, CFM_WIDTH), 0.02),
        "beta_mix": 1.0 + nrm(ks[13], (L, MIX_WIDTH), 0.02),
        "w_out": nrm(ks[14], (L, MIX_WIDTH, D_MODEL), MIX_WIDTH ** -0.5),
        "g_pre_ffn": 1.0 + nrm(ks[15], (L, D_MODEL), 0.02),
        "g_post_ffn": 1.0 + nrm(ks[16], (L, D_MODEL), 0.02),
        "w_gate_up": nrm(ks[17], (L, D_MODEL, 2 * FFN_HIDDEN), D_MODEL ** -0.5),
        "w_down": nrm(ks[18], (L, FFN_HIDDEN, D_MODEL), FFN_HIDDEN ** -0.5),
    }


def token_mixer(h, w_in, w_short, b_short, w_cfm_dw, b_cfm_dw, g_cfm_ln, b_cfm_ln,
                beta_mix, w_out):
    z = jnp.einsum("bsd,dk->bsk", h, w_in)
    gb, gc, v, a, g = jnp.split(
        z, [CONV_WIDTH, 2 * CONV_WIDTH, 3 * CONV_WIDTH, 3 * CONV_WIDTH + CFM_WIDTH], axis=-1)
    y_short = gb * causal_dwconv(gc * v, w_short, b_short)
    u = a * jax.nn.sigmoid(g)
    u = causal_dwconv(u, w_cfm_dw, b_cfm_dw)
    u = layernorm(u, g_cfm_ln, b_cfm_ln)
    y_cfm = jax.nn.silu(u)
    y = jnp.concatenate([y_short, y_cfm], axis=-1)
    bsz, s = y.shape[0], y.shape[1]
    yh = y.reshape(bsz, s, N_HEADS, HEAD_DIM).astype(jnp.float32)
    yh = yh * lax.rsqrt(jnp.mean(yh * yh, axis=-1, keepdims=True) + EPS)
    y = (yh.reshape(bsz, s, MIX_WIDTH) * beta_mix.astype(jnp.float32)).astype(h.dtype)
    return jnp.einsum("bsm,md->bsd", y, w_out)


def swiglu_ffn(h, w_gate_up, w_down):
    gu = jnp.einsum("bsd,df->bsf", h, w_gate_up)
    gate, up = jnp.split(gu, 2, axis=-1)
    return jnp.einsum("bsf,fd->bsd", jax.nn.silu(gate) * up, w_down)


def reference(x, c, w_ada, b_ada, g_pre_mix, g_post_mix, w_in, w_short, b_short,
              w_cfm_dw, b_cfm_dw, g_cfm_ln, b_cfm_ln, beta_mix, w_out,
              g_pre_ffn, g_post_ffn, w_gate_up, w_down):
    c_act = jax.nn.silu(c)
    for l in range(DEPTH):
        mod = jnp.einsum("bd,dk->bk", c_act, w_ada[l]) + b_ada[l]
        sh1, sc1, gt1, sh2, sc2, gt2 = jnp.split(mod, N_MOD, axis=-1)
        h = modulate(rmsnorm(x, g_pre_mix[l]), sh1, sc1)
        o = token_mixer(h, w_in[l], w_short[l], b_short[l], w_cfm_dw[l], b_cfm_dw[l],
                        g_cfm_ln[l], b_cfm_ln[l], beta_mix[l], w_out[l])
        x = x + gt1[:, None, :] * rmsnorm(o, g_post_mix[l])
        h = modulate(rmsnorm(x, g_pre_ffn[l]), sh2, sc2)
        f = swiglu_ffn(h, w_gate_up[l], w_down[l])
        x = x + gt2[:, None, :] * rmsnorm(f, g_post_ffn[l])
    return x
```

```python
import functools

import jax
import jax.numpy as jnp
from jax.experimental import pallas as pl
from jax.experimental.pallas import tpu as pltpu

EPS = 1e-6
HEAD_DIM = 64
SHORT_K = 3
CFM_K = 31

V7X_SUBLANES = 8
V7X_LANES = 128

TIME_TILE = 64
CONV_TIME = 8
FFN_CHUNK = 256
ADA_COLS = 512
VMEM_LIMIT_BYTES = 56 * 1024 * 1024


def _rms(v, g):
    ms = jnp.mean(v * v, axis=-1, keepdims=True)
    return v * jax.lax.rsqrt(ms + EPS) * g


def _sigmoid(v):
    return 1.0 / (1.0 + jnp.exp(-v))


def _adaln_kernel(c_ref, w_ref, b_ref, o_ref):
    c = c_ref[...]
    c_act = (c * _sigmoid(c)).astype(jnp.bfloat16)
    acc = jnp.dot(c_act, w_ref[...].astype(jnp.bfloat16), preferred_element_type=jnp.float32)
    o_ref[...] = acc + b_ref[...]


def _per_batch(v, m):
    nb = m.shape[0]
    return v.reshape(v.shape[0] // nb, nb, v.shape[1]), m[None]


def _modulate(v, shift, scale):
    v3, sc3 = _per_batch(v, scale)
    return (v3 * (1.0 + sc3) + shift[None]).reshape(v.shape)


def _gated_residual(x, gate, v):
    v3, g3 = _per_batch(v, gate)
    return x + (g3 * v3).reshape(x.shape)


def _causal_dwconv(src_buf, w_ref, b_ref, dst_buf, n_taps, nb):
    rows = dst_buf.shape[0]
    width = dst_buf.shape[1]
    blk = CONV_TIME * nb
    halo = nb * (n_taps - 1)

    def row_block(i, carry):
        r0 = pl.multiple_of(i * blk, blk)
        for c0 in range(0, width, V7X_LANES):
            win = src_buf[pl.ds(r0, blk + halo), c0:c0 + V7X_LANES]
            acc = jnp.broadcast_to(b_ref[:, c0:c0 + V7X_LANES], (blk, V7X_LANES))
            for k in range(n_taps):
                acc = acc + w_ref[k:k + 1, c0:c0 + V7X_LANES] * win[nb * k:nb * k + blk]
            dst_buf[pl.ds(r0, blk), c0:c0 + V7X_LANES] = acc
        return carry

    jax.lax.fori_loop(0, rows // blk, row_block, 0)


def _layer_kernel(x_ref, mod_ref, g_pre_mix_ref, g_post_mix_ref, w_in_ref, w_short_ref, b_short_ref,
                  w_cfm_ref, b_cfm_ref, g_ln_ref, b_ln_ref, beta_ref, seg_ref, w_out_ref,
                  g_pre_ffn_ref, g_post_ffn_ref, w_gu_ref, w_down_ref,
                  o_ref,
                  p_buf, u_buf, conv_s, conv_c, f_acc,
                  *, conv_w, ffn_hidden, nb):
    step = pl.program_id(0)
    T, D = x_ref.shape
    bf16 = jnp.bfloat16
    f32 = jnp.float32
    p_halo = nb * (SHORT_K - 1)
    u_halo = nb * (CFM_K - 1)

    x = x_ref[...]
    mod = mod_ref[...]
    sh1, sc1, gt1, sh2, sc2, gt2 = (mod[:, i * D:(i + 1) * D] for i in range(6))

    h = _modulate(_rms(x, g_pre_mix_ref[...]), sh1, sc1)
    z = jnp.dot(h.astype(bf16), w_in_ref[...], preferred_element_type=f32)
    gb = z[:, 0 * conv_w:1 * conv_w]
    gc = z[:, 1 * conv_w:2 * conv_w]
    v = z[:, 2 * conv_w:3 * conv_w]
    a = z[:, 3 * conv_w:4 * conv_w]
    g = z[:, 4 * conv_w:5 * conv_w]

    @pl.when(step == 0)
    def _():
        p_buf[0:p_halo, :] = jnp.zeros((p_halo, conv_w), f32)
        u_buf[0:u_halo, :] = jnp.zeros((u_halo, conv_w), f32)

    @pl.when(step > 0)
    def _():
        p_buf[0:p_halo, :] = p_buf[T:T + p_halo, :]
        u_buf[0:u_halo, :] = u_buf[T:T + u_halo, :]

    p_buf[p_halo:p_halo + T, :] = gc * v
    u_buf[u_halo:u_halo + T, :] = a * _sigmoid(g)

    _causal_dwconv(p_buf, w_short_ref, b_short_ref, conv_s, SHORT_K, nb)
    _causal_dwconv(u_buf, w_cfm_ref, b_cfm_ref, conv_c, CFM_K, nb)

    y_short = gb * conv_s[...]
    uc = conv_c[...]
    mu = jnp.mean(uc, axis=-1, keepdims=True)
    d = uc - mu
    var = jnp.mean(d * d, axis=-1, keepdims=True)
    ln = d * jax.lax.rsqrt(var + EPS) * g_ln_ref[...] + b_ln_ref[...]
    y_cfm = ln * _sigmoid(ln)

    y = jnp.concatenate([y_short, y_cfm], axis=-1)
    ms = jnp.dot((y * y).astype(bf16), seg_ref[...], preferred_element_type=f32)
    yn = y * jax.lax.rsqrt(ms + EPS) * beta_ref[...]
    o = jnp.dot(yn.astype(bf16), w_out_ref[...], preferred_element_type=f32)
    x1 = _gated_residual(x, gt1, _rms(o, g_post_mix_ref[...]))

    h2 = _modulate(_rms(x1, g_pre_ffn_ref[...]), sh2, sc2).astype(bf16)
    for j in range(ffn_hidden // FFN_CHUNK):
        c0 = j * FFN_CHUNK
        gate = jnp.dot(h2, w_gu_ref[:, c0:c0 + FFN_CHUNK], preferred_element_type=f32)
        up = jnp.dot(h2, w_gu_ref[:, ffn_hidden + c0:ffn_hidden + c0 + FFN_CHUNK], preferred_element_type=f32)
        act = (gate * _sigmoid(gate) * up).astype(bf16)
        part = jnp.dot(act, w_down_ref[c0:c0 + FFN_CHUNK, :], preferred_element_type=f32)
        if j == 0:
            f_acc[...] = part
        else:
            f_acc[...] += part
    o_ref[...] = _gated_residual(x1, gt2, _rms(f_acc[...], g_post_ffn_ref[...]))


def _const_spec(shape):
    zeros = (0,) * len(shape)
    return pl.BlockSpec(shape, lambda s: zeros, pipeline_mode=pl.Buffered(1))


def kernel(x, c, w_ada, b_ada, g_pre_mix, g_post_mix, w_in, w_short, b_short, w_cfm_dw, b_cfm_dw,
           g_cfm_ln, b_cfm_ln, beta_mix, w_out, g_pre_ffn, g_post_ffn, w_gate_up, w_down):
    B, S, D = x.shape
    assert w_ada.shape[0] == 1, "single-layer block"
    n_mod = w_ada.shape[2] // D
    conv_w = w_short.shape[2]
    ffn_hidden = w_down.shape[1]
    assert B == V7X_SUBLANES and n_mod == 6
    assert w_in.shape[2] == 5 * conv_w and 2 * conv_w == D
    assert S % TIME_TILE == 0 and TIME_TILE % CONV_TIME == 0
    assert ffn_hidden % FFN_CHUNK == 0 and (n_mod * D) % ADA_COLS == 0
    bf16 = jnp.bfloat16
    T = TIME_TILE * B

    mod = pl.pallas_call(
        _adaln_kernel,
        out_shape=jax.ShapeDtypeStruct((B, n_mod * D), jnp.float32),
        grid=(n_mod * D // ADA_COLS,),
        in_specs=[
            pl.BlockSpec((B, D), lambda j: (0, 0)),
            pl.BlockSpec((D, ADA_COLS), lambda j: (0, j)),
            pl.BlockSpec((1, ADA_COLS), lambda j: (0, j)),
        ],
        out_specs=pl.BlockSpec((B, ADA_COLS), lambda j: (0, j)),
        compiler_params=pltpu.CompilerParams(dimension_semantics=("arbitrary",)),
        name="adaln",
    )(c, w_ada[0], b_ada)

    head = jnp.arange(D, dtype=jnp.int32) // HEAD_DIM
    seg = jnp.where(head[:, None] == head[None, :], 1.0 / HEAD_DIM, 0.0).astype(bf16)

    x_tb = jnp.swapaxes(x, 0, 1).reshape(S * B, D)
    operands = (
        x_tb, mod, g_pre_mix, g_post_mix, w_in[0].astype(bf16), w_short[0], b_short,
        w_cfm_dw[0], b_cfm_dw, g_cfm_ln, b_cfm_ln, beta_mix, seg, w_out[0].astype(bf16),
        g_pre_ffn, g_post_ffn, w_gate_up[0].astype(bf16), w_down[0].astype(bf16),
    )
    in_specs = [pl.BlockSpec((T, D), lambda s: (s, 0))] + [_const_spec(t.shape) for t in operands[1:]]

    out_tb = pl.pallas_call(
        functools.partial(_layer_kernel, conv_w=conv_w, ffn_hidden=ffn_hidden, nb=B),
        out_shape=jax.ShapeDtypeStruct((S * B, D), x.dtype),
        grid=(S // TIME_TILE,),
        in_specs=in_specs,
        out_specs=pl.BlockSpec((T, D), lambda s: (s, 0)),
        scratch_shapes=[
            pltpu.VMEM((B * (SHORT_K - 1) + T, conv_w), jnp.float32),
            pltpu.VMEM((B * (CFM_K - 1) + T, conv_w), jnp.float32),
            pltpu.VMEM((T, conv_w), jnp.float32),
            pltpu.VMEM((T, conv_w), jnp.float32),
            pltpu.VMEM((T, D), jnp.float32),
        ],
        compiler_params=pltpu.CompilerParams(
            dimension_semantics=("arbitrary",),
            vmem_limit_bytes=VMEM_LIMIT_BYTES,
        ),
        name="layer",
    )(*operands)
    return jnp.swapaxes(out_tb.reshape(S, B, D), 0, 1)
```

```python
import functools

import jax
import jax.numpy as jnp
from jax.experimental import pallas as pl
from jax.experimental.pallas import tpu as pltpu

EPS = 1e-6
HEAD_DIM = 64
SHORT_K = 3
CFM_K = 31

V7X_SUBLANES = 8
V7X_LANES = 128

TIME_TILE = 64
CONV_TIME = 8
FFN_CHUNK = 256
ADA_COLS = 512
VMEM_LIMIT_BYTES = 56 * 1024 * 1024


def _rms(v, g):
    ms = jnp.mean(v * v, axis=-1, keepdims=True)
    return v * jax.lax.rsqrt(ms + EPS) * g


def _sigmoid(v):
    return 1.0 / (1.0 + jnp.exp(-v))


def _adaln_kernel(c_ref, w_ref, b_ref, o_ref):
    c = c_ref[...]
    c_act = (c * _sigmoid(c)).astype(jnp.bfloat16)
    acc = jnp.dot(c_act, w_ref[...].astype(jnp.bfloat16), preferred_element_type=jnp.float32)
    o_ref[...] = acc + b_ref[...]


def _per_batch(v, m):
    nb = m.shape[0]
    return v.reshape(v.shape[0] // nb, nb, v.shape[1]), m[None]


def _modulate(v, shift, scale):
    v3, sc3 = _per_batch(v, scale)
    return (v3 * (1.0 + sc3) + shift[None]).reshape(v.shape)


def _gated_residual(x, gate, v):
    v3, g3 = _per_batch(v, gate)
    return x + (g3 * v3).reshape(x.shape)


def _causal_dwconv(src_buf, w_ref, b_ref, dst_buf, n_taps, nb):
    rows = dst_buf.shape[0]
    width = dst_buf.shape[1]
    blk = CONV_TIME * nb
    halo = nb * (n_taps - 1)

    def row_block(i, carry):
        r0 = pl.multiple_of(i * blk, blk)
        for c0 in range(0, width, V7X_LANES):
            win = src_buf[pl.ds(r0, blk + halo), c0:c0 + V7X_LANES]
            acc = jnp.broadcast_to(b_ref[:, c0:c0 + V7X_LANES], (blk, V7X_LANES))
            for k in range(n_taps):
                acc = acc + w_ref[k:k + 1, c0:c0 + V7X_LANES] * win[nb * k:nb * k + blk]
            dst_buf[pl.ds(r0, blk), c0:c0 + V7X_LANES] = acc
        return carry

    jax.lax.fori_loop(0, rows // blk, row_block, 0)


def _tile_copies(hbm_ref, buf, sem, tile, slot, to_vmem):
    nt, nb = buf.shape[1], buf.shape[2]
    copies = []
    for b in range(nb):
        hbm = hbm_ref.at[b, pl.ds(tile * nt, nt), :]
        vmem = buf.at[slot, :, b, :]
        src, dst = (hbm, vmem) if to_vmem else (vmem, hbm)
        copies.append(pltpu.make_async_copy(src, dst, sem.at[slot]))
    return copies


def _layer_kernel(x_hbm, mod_ref, g_pre_mix_ref, g_post_mix_ref, w_in_ref, w_short_ref, b_short_ref,
                  w_cfm_ref, b_cfm_ref, g_ln_ref, b_ln_ref, beta_ref, seg_ref, w_out_ref,
                  g_pre_ffn_ref, g_post_ffn_ref, w_gu_ref, w_down_ref,
                  o_hbm,
                  x_buf, o_buf, in_sem, out_sem, p_buf, u_buf, conv_s, conv_c, f_acc,
                  *, conv_w, ffn_hidden, nb, n_steps):
    step = pl.program_id(0)
    slot = jax.lax.rem(step, 2)
    nt, D = x_buf.shape[1], x_buf.shape[3]
    T = nt * nb
    bf16 = jnp.bfloat16
    f32 = jnp.float32
    p_halo = nb * (SHORT_K - 1)
    u_halo = nb * (CFM_K - 1)

    @pl.when(step == 0)
    def _():
        for cp in _tile_copies(x_hbm, x_buf, in_sem, step, slot, True):
            cp.start()

    @pl.when(step + 1 < n_steps)
    def _():
        for cp in _tile_copies(x_hbm, x_buf, in_sem, step + 1, 1 - slot, True):
            cp.start()

    for cp in _tile_copies(x_hbm, x_buf, in_sem, step, slot, True):
        cp.wait()

    x = x_buf[slot].reshape(T, D)
    mod = mod_ref[...]
    sh1, sc1, gt1, sh2, sc2, gt2 = (mod[:, i * D:(i + 1) * D] for i in range(6))

    h = _modulate(_rms(x, g_pre_mix_ref[...]), sh1, sc1)
    z = jnp.dot(h.astype(bf16), w_in_ref[...], preferred_element_type=f32)
    gb = z[:, 0 * conv_w:1 * conv_w]
    gc = z[:, 1 * conv_w:2 * conv_w]
    v = z[:, 2 * conv_w:3 * conv_w]
    a = z[:, 3 * conv_w:4 * conv_w]
    g = z[:, 4 * conv_w:5 * conv_w]

    @pl.when(step == 0)
    def _():
        p_buf[0:p_halo, :] = jnp.zeros((p_halo, conv_w), f32)
        u_buf[0:u_halo, :] = jnp.zeros((u_halo, conv_w), f32)

    @pl.when(step > 0)
    def _():
        p_buf[0:p_halo, :] = p_buf[T:T + p_halo, :]
        u_buf[0:u_halo, :] = u_buf[T:T + u_halo, :]

    p_buf[p_halo:p_halo + T, :] = gc * v
    u_buf[u_halo:u_halo + T, :] = a * _sigmoid(g)

    _causal_dwconv(p_buf, w_short_ref, b_short_ref, conv_s, SHORT_K, nb)
    _causal_dwconv(u_buf, w_cfm_ref, b_cfm_ref, conv_c, CFM_K, nb)

    y_short = gb * conv_s[...]
    uc = conv_c[...]
    mu = jnp.mean(uc, axis=-1, keepdims=True)
    d = uc - mu
    var = jnp.mean(d * d, axis=-1, keepdims=True)
    ln = d * jax.lax.rsqrt(var + EPS) * g_ln_ref[...] + b_ln_ref[...]
    y_cfm = ln * _sigmoid(ln)

    y = jnp.concatenate([y_short, y_cfm], axis=-1)
    ms = jnp.dot((y * y).astype(bf16), seg_ref[...], preferred_element_type=f32)
    yn = y * jax.lax.rsqrt(ms + EPS) * beta_ref[...]
    o = jnp.dot(yn.astype(bf16), w_out_ref[...], preferred_element_type=f32)
    x1 = _gated_residual(x, gt1, _rms(o, g_post_mix_ref[...]))

    h2 = _modulate(_rms(x1, g_pre_ffn_ref[...]), sh2, sc2).astype(bf16)
    for j in range(ffn_hidden // FFN_CHUNK):
        c0 = j * FFN_CHUNK
        gate = jnp.dot(h2, w_gu_ref[:, c0:c0 + FFN_CHUNK], preferred_element_type=f32)
        up = jnp.dot(h2, w_gu_ref[:, ffn_hidden + c0:ffn_hidden + c0 + FFN_CHUNK], preferred_element_type=f32)
        act = (gate * _sigmoid(gate) * up).astype(bf16)
        part = jnp.dot(act, w_down_ref[c0:c0 + FFN_CHUNK, :], preferred_element_type=f32)
        if j == 0:
            f_acc[...] = part
        else:
            f_acc[...] += part
    out = _gated_residual(x1, gt2, _rms(f_acc[...], g_post_ffn_ref[...]))

    @pl.when(step >= 2)
    def _():
        for cp in _tile_copies(o_hbm, o_buf, out_sem, step - 2, slot, False):
            cp.wait()

    o_buf[slot] = out.reshape(nt, nb, D)
    for cp in _tile_copies(o_hbm, o_buf, out_sem, step, slot, False):
        cp.start()

    @pl.when(step == n_steps - 1)
    def _():
        if n_steps >= 2:
            for cp in _tile_copies(o_hbm, o_buf, out_sem, step - 1, 1 - slot, False):
                cp.wait()
        for cp in _tile_copies(o_hbm, o_buf, out_sem, step, slot, False):
            cp.wait()


def _const_spec(shape):
    zeros = (0,) * len(shape)
    return pl.BlockSpec(shape, lambda s: zeros, pipeline_mode=pl.Buffered(1))


def kernel(x, c, w_ada, b_ada, g_pre_mix, g_post_mix, w_in, w_short, b_short, w_cfm_dw, b_cfm_dw,
           g_cfm_ln, b_cfm_ln, beta_mix, w_out, g_pre_ffn, g_post_ffn, w_gate_up, w_down):
    B, S, D = x.shape
    assert w_ada.shape[0] == 1, "single-layer block"
    n_mod = w_ada.shape[2] // D
    conv_w = w_short.shape[2]
    ffn_hidden = w_down.shape[1]
    assert B == V7X_SUBLANES and n_mod == 6
    assert w_in.shape[2] == 5 * conv_w and 2 * conv_w == D
    assert S % TIME_TILE == 0 and TIME_TILE % CONV_TIME == 0
    assert ffn_hidden % FFN_CHUNK == 0 and (n_mod * D) % ADA_COLS == 0
    bf16 = jnp.bfloat16
    T = TIME_TILE * B

    mod = pl.pallas_call(
        _adaln_kernel,
        out_shape=jax.ShapeDtypeStruct((B, n_mod * D), jnp.float32),
        grid=(n_mod * D // ADA_COLS,),
        in_specs=[
            pl.BlockSpec((B, D), lambda j: (0, 0)),
            pl.BlockSpec((D, ADA_COLS), lambda j: (0, j)),
            pl.BlockSpec((1, ADA_COLS), lambda j: (0, j)),
        ],
        out_specs=pl.BlockSpec((B, ADA_COLS), lambda j: (0, j)),
        compiler_params=pltpu.CompilerParams(dimension_semantics=("arbitrary",)),
        name="adaln",
    )(c, w_ada[0], b_ada)

    head = jnp.arange(D, dtype=jnp.int32) // HEAD_DIM
    seg = jnp.where(head[:, None] == head[None, :], 1.0 / HEAD_DIM, 0.0).astype(bf16)

    operands = (
        x, mod, g_pre_mix, g_post_mix, w_in[0].astype(bf16), w_short[0], b_short,
        w_cfm_dw[0], b_cfm_dw, g_cfm_ln, b_cfm_ln, beta_mix, seg, w_out[0].astype(bf16),
        g_pre_ffn, g_post_ffn, w_gate_up[0].astype(bf16), w_down[0].astype(bf16),
    )
    in_specs = [pl.BlockSpec(memory_space=pl.ANY)] + [_const_spec(t.shape) for t in operands[1:]]
    n_steps = S // TIME_TILE

    return pl.pallas_call(
        functools.partial(_layer_kernel, conv_w=conv_w, ffn_hidden=ffn_hidden, nb=B, n_steps=n_steps),
        out_shape=jax.ShapeDtypeStruct((B, S, D), x.dtype),
        grid=(n_steps,),
        in_specs=in_specs,
        out_specs=pl.BlockSpec(memory_space=pl.ANY),
        scratch_shapes=[
            pltpu.VMEM((2, TIME_TILE, B, D), jnp.float32),
            pltpu.VMEM((2, TIME_TILE, B, D), jnp.float32),
            pltpu.SemaphoreType.DMA((2,)),
            pltpu.SemaphoreType.DMA((2,)),
            pltpu.VMEM((B * (SHORT_K - 1) + T, conv_w), jnp.float32),
            pltpu.VMEM((B * (CFM_K - 1) + T, conv_w), jnp.float32),
            pltpu.VMEM((T, conv_w), jnp.float32),
            pltpu.VMEM((T, conv_w), jnp.float32),
            pltpu.VMEM((T, D), jnp.float32),
        ],
        compiler_params=pltpu.CompilerParams(
            dimension_semantics=("arbitrary",),
            vmem_limit_bytes=VMEM_LIMIT_BYTES,
        ),
        name="layer",
    )(*operands)
```

```python
import functools

import jax
import jax.numpy as jnp
from jax.experimental import pallas as pl
from jax.experimental.pallas import tpu as pltpu

EPS = 1e-6
HEAD_DIM = 64
SHORT_K = 3
CFM_K = 31

V7X_SUBLANES = 8
V7X_LANES = 128

TIME_TILE = 64
CONV_TIME = 8
CONV_TRIPS = 4
FFN_CHUNK = 256
FFN_HEAD_CHUNKS = 2
ADA_COLS = 512
VMEM_LIMIT_BYTES = 60 * 1024 * 1024


def _rms(v, g):
    ms = jnp.mean(v * v, axis=-1, keepdims=True)
    return v * jax.lax.rsqrt(ms + EPS) * g


def _sigmoid(v):
    return 1.0 / (1.0 + jnp.exp(-v))


def _dot(a, b):
    return jnp.dot(a, b, preferred_element_type=jnp.float32)


def _adaln_kernel(c_ref, w_ref, b_ref, o_ref):
    c = c_ref[...]
    c_act = (c * _sigmoid(c)).astype(jnp.bfloat16)
    o_ref[...] = _dot(c_act, w_ref[...].astype(jnp.bfloat16)) + b_ref[...]


def _per_batch(v, m):
    nb = m.shape[0]
    return v.reshape(v.shape[0] // nb, nb, v.shape[1]), m[None]


def _modulate(v, shift, scale):
    v3, sc3 = _per_batch(v, scale)
    return (v3 * (1.0 + sc3) + shift[None]).reshape(v.shape)


def _gated_residual(x, gate, v):
    v3, g3 = _per_batch(v, gate)
    return x + (g3 * v3).reshape(x.shape)


def _dwconv_block(src_buf, w_ref, b_ref, r0, c0, n_taps, nb):
    blk = CONV_TIME * nb
    win = src_buf[pl.ds(r0, blk + nb * (n_taps - 1)), c0:c0 + V7X_LANES]
    acc = jnp.broadcast_to(b_ref[:, c0:c0 + V7X_LANES], (blk, V7X_LANES))
    for k in range(n_taps):
        acc = acc + w_ref[k:k + 1, c0:c0 + V7X_LANES] * win[nb * k:nb * k + blk]
    return acc


def _tile_copies(hbm_ref, buf, sem, tile, slot, to_vmem):
    nt, nb = buf.shape[1], buf.shape[2]
    copies = []
    for b in range(nb):
        hbm = hbm_ref.at[b, pl.ds(tile * nt, nt), :]
        vmem = buf.at[slot, :, b, :]
        src, dst = (hbm, vmem) if to_vmem else (vmem, hbm)
        copies.append(pltpu.make_async_copy(src, dst, sem.at[slot]))
    return copies


def _layer_kernel(x_hbm, mod_ref, g_pre_mix_ref, g_post_mix_ref, w_in_ref, w_short_ref, b_short_ref,
                  w_cfm_ref, b_cfm_ref, g_ln_ref, b_ln_ref, beta_ref, seg_ref, w_out_ref,
                  g_pre_ffn_ref, g_post_ffn_ref, w_gu_ref, w_down_ref,
                  o_hbm,
                  x_buf, o_buf, in_sem, out_sem, p_buf, u_buf, gb_buf, y_buf, f_acc, act_buf,
                  x1_buf, h2_buf,
                  *, conv_w, ffn_hidden, nb, n_tiles):
    step = pl.program_id(0)
    slot = jax.lax.rem(step, 2)
    nt, D = x_buf.shape[1], x_buf.shape[3]
    T = nt * nb
    bf16 = jnp.bfloat16
    f32 = jnp.float32
    p_halo = nb * (SHORT_K - 1)
    u_halo = nb * (CFM_K - 1)
    blk = CONV_TIME * nb
    n_chunks = ffn_hidden // FFN_CHUNK

    @pl.when(step == 0)
    def _():
        for cp in _tile_copies(x_hbm, x_buf, in_sem, step, slot, True):
            cp.start()
        x1_buf[1 - slot] = jnp.zeros(x1_buf.shape[1:], f32)
        h2_buf[1 - slot] = jnp.zeros(h2_buf.shape[1:], bf16)
        p_buf[0:p_halo, :] = jnp.zeros((p_halo, conv_w), f32)
        u_buf[0:u_halo, :] = jnp.zeros((u_halo, conv_w), f32)

    @pl.when(step > 0)
    def _():
        p_buf[0:p_halo, :] = p_buf[T:T + p_halo, :]
        u_buf[0:u_halo, :] = u_buf[T:T + u_halo, :]

    @pl.when(step + 1 < n_tiles)
    def _():
        for cp in _tile_copies(x_hbm, x_buf, in_sem, step + 1, 1 - slot, True):
            cp.start()

    @pl.when(step < n_tiles)
    def _():
        for cp in _tile_copies(x_hbm, x_buf, in_sem, step, slot, True):
            cp.wait()

    @pl.when(step >= 3)
    def _():
        for cp in _tile_copies(o_hbm, o_buf, out_sem, step - 3, 1 - slot, False):
            cp.wait()

    mod = mod_ref[...]
    sh1, sc1, gt1, sh2, sc2, gt2 = (mod[:, i * D:(i + 1) * D] for i in range(6))
    x1_prev, h2_prev = x1_buf.at[1 - slot], h2_buf.at[1 - slot]

    def ffn_act(c0):
        h2 = h2_prev[...]
        gate = _dot(h2, w_gu_ref[:, pl.ds(c0, FFN_CHUNK)])
        up = _dot(h2, w_gu_ref[:, pl.ds(ffn_hidden + c0, FFN_CHUNK)])
        return (gate * _sigmoid(gate) * up).astype(bf16)

    def ffn_down(act, c0):
        return _dot(act, w_down_ref[pl.ds(c0, FFN_CHUNK), :])

    def ffn_chunk(j):
        return ffn_down(ffn_act(j * FFN_CHUNK), j * FFN_CHUNK)

    head = FFN_HEAD_CHUNKS
    assert head + CONV_TRIPS + 1 <= n_chunks

    f_acc[...] = ffn_chunk(0)
    x = x_buf[slot].reshape(T, D)
    hb = _modulate(_rms(x, g_pre_mix_ref[...]), sh1, sc1).astype(bf16)
    z = _dot(hb, w_in_ref[:, 1 * conv_w:3 * conv_w])
    p_buf[p_halo:p_halo + T, :] = z[:, :conv_w] * z[:, conv_w:]
    z = _dot(hb, w_in_ref[:, 3 * conv_w:5 * conv_w])
    u_buf[u_halo:u_halo + T, :] = z[:, :conv_w] * _sigmoid(z[:, conv_w:])
    gb_buf[...] = _dot(hb, w_in_ref[:, 0:conv_w])
    for j in range(1, head):
        f_acc[...] += ffn_chunk(j)
    act_buf[...] = ffn_act(head * FFN_CHUNK)

    rows_per_trip = T // CONV_TRIPS

    def conv_and_ffn(i, carry):
        c_dn = pl.multiple_of((head + i) * FFN_CHUNK, FFN_CHUNK)
        act_next = ffn_act(c_dn + FFN_CHUNK)
        part = ffn_down(act_buf[...], c_dn)
        for rb in range(rows_per_trip // blk):
            r0 = pl.multiple_of(i * rows_per_trip + rb * blk, blk)
            for c0 in range(0, conv_w, V7X_LANES):
                conv = _dwconv_block(p_buf, w_short_ref, b_short_ref, r0, c0, SHORT_K, nb)
                y_buf[pl.ds(r0, blk), c0:c0 + V7X_LANES] = gb_buf[pl.ds(r0, blk), c0:c0 + V7X_LANES] * conv
                y_buf[pl.ds(r0, blk), conv_w + c0:conv_w + c0 + V7X_LANES] = _dwconv_block(
                    u_buf, w_cfm_ref, b_cfm_ref, r0, c0, CFM_K, nb)
        f_acc[...] += part
        act_buf[...] = act_next
        return carry

    jax.lax.fori_loop(0, CONV_TRIPS, conv_and_ffn, 0)

    nxt = head + CONV_TRIPS
    tail = list(range(nxt + 1, n_chunks))
    split = (len(tail) + 1) // 2
    f_acc[...] += ffn_down(act_buf[...], nxt * FFN_CHUNK)
    for j in tail[:split]:
        f_acc[...] += ffn_chunk(j)

    y_short = y_buf[:, :conv_w]
    uc = y_buf[:, conv_w:]
    mu = jnp.mean(uc, axis=-1, keepdims=True)
    d = uc - mu
    var = jnp.mean(d * d, axis=-1, keepdims=True)
    ln = d * jax.lax.rsqrt(var + EPS) * g_ln_ref[...] + b_ln_ref[...]
    y_cfm = ln * _sigmoid(ln)
    y = jnp.concatenate([y_short, y_cfm], axis=-1)
    ms = _dot((y * y).astype(bf16), seg_ref[...])
    yn = y * jax.lax.rsqrt(ms + EPS) * beta_ref[...]
    o = _dot(yn.astype(bf16), w_out_ref[...])

    for j in tail[split:]:
        f_acc[...] += ffn_chunk(j)

    x1 = _gated_residual(x_buf[slot].reshape(T, D), gt1, _rms(o, g_post_mix_ref[...]))
    x1_buf[slot] = x1
    h2_buf[slot] = _modulate(_rms(x1, g_pre_ffn_ref[...]), sh2, sc2).astype(bf16)

    out = _gated_residual(x1_prev[...], gt2, _rms(f_acc[...], g_post_ffn_ref[...]))
    o_buf[1 - slot] = out.reshape(nt, nb, D)

    @pl.when(step >= 1)
    def _():
        for cp in _tile_copies(o_hbm, o_buf, out_sem, step - 1, 1 - slot, False):
            cp.start()

    @pl.when(step == n_tiles)
    def _():
        if n_tiles >= 2:
            for cp in _tile_copies(o_hbm, o_buf, out_sem, step - 2, slot, False):
                cp.wait()
        for cp in _tile_copies(o_hbm, o_buf, out_sem, step - 1, 1 - slot, False):
            cp.wait()


def _const_spec(shape):
    zeros = (0,) * len(shape)
    return pl.BlockSpec(shape, lambda s: zeros, pipeline_mode=pl.Buffered(1))


def kernel(x, c, w_ada, b_ada, g_pre_mix, g_post_mix, w_in, w_short, b_short, w_cfm_dw, b_cfm_dw,
           g_cfm_ln, b_cfm_ln, beta_mix, w_out, g_pre_ffn, g_post_ffn, w_gate_up, w_down):
    B, S, D = x.shape
    assert w_ada.shape[0] == 1, "single-layer block"
    n_mod = w_ada.shape[2] // D
    conv_w = w_short.shape[2]
    ffn_hidden = w_down.shape[1]
    assert B == V7X_SUBLANES and n_mod == 6
    assert w_in.shape[2] == 5 * conv_w and 2 * conv_w == D
    assert S % TIME_TILE == 0 and TIME_TILE % (CONV_TRIPS * CONV_TIME) == 0
    assert ffn_hidden % FFN_CHUNK == 0 and (n_mod * D) % ADA_COLS == 0
    bf16 = jnp.bfloat16
    T = TIME_TILE * B

    mod = pl.pallas_call(
        _adaln_kernel,
        out_shape=jax.ShapeDtypeStruct((B, n_mod * D), jnp.float32),
        grid=(n_mod * D // ADA_COLS,),
        in_specs=[
            pl.BlockSpec((B, D), lambda j: (0, 0)),
            pl.BlockSpec((D, ADA_COLS), lambda j: (0, j)),
            pl.BlockSpec((1, ADA_COLS), lambda j: (0, j)),
        ],
        out_specs=pl.BlockSpec((B, ADA_COLS), lambda j: (0, j)),
        compiler_params=pltpu.CompilerParams(dimension_semantics=("arbitrary",)),
        name="adaln",
    )(c, w_ada[0], b_ada)

    head = jnp.arange(D, dtype=jnp.int32) // HEAD_DIM
    seg = jnp.where(head[:, None] == head[None, :], 1.0 / HEAD_DIM, 0.0).astype(bf16)

    operands = (
        x, mod, g_pre_mix, g_post_mix, w_in[0].astype(bf16), w_short[0], b_short,
        w_cfm_dw[0], b_cfm_dw, g_cfm_ln, b_cfm_ln, beta_mix, seg, w_out[0].astype(bf16),
        g_pre_ffn, g_post_ffn, w_gate_up[0].astype(bf16), w_down[0].astype(bf16),
    )
    in_specs = [pl.BlockSpec(memory_space=pl.ANY)] + [_const_spec(t.shape) for t in operands[1:]]
    n_tiles = S // TIME_TILE

    return pl.pallas_call(
        functools.partial(_layer_kernel, conv_w=conv_w, ffn_hidden=ffn_hidden, nb=B, n_tiles=n_tiles),
        out_shape=jax.ShapeDtypeStruct((B, S, D), x.dtype),
        grid=(n_tiles + 1,),
        in_specs=in_specs,
        out_specs=pl.BlockSpec(memory_space=pl.ANY),
        scratch_shapes=[
            pltpu.VMEM((2, TIME_TILE, B, D), jnp.float32),
            pltpu.VMEM((2, TIME_TILE, B, D), jnp.float32),
            pltpu.SemaphoreType.DMA((2,)),
            pltpu.SemaphoreType.DMA((2,)),
            pltpu.VMEM((B * (SHORT_K - 1) + T, conv_w), jnp.float32),
            pltpu.VMEM((B * (CFM_K - 1) + T, conv_w), jnp.float32),
            pltpu.VMEM((T, conv_w), jnp.float32),
            pltpu.VMEM((T, D), jnp.float32),
            pltpu.VMEM((T, D), jnp.float32),
            pltpu.VMEM((T, FFN_CHUNK), jnp.bfloat16),
            pltpu.VMEM((2, T, D), jnp.float32),
            pltpu.VMEM((2, T, D), jnp.bfloat16),
        ],
        compiler_params=pltpu.CompilerParams(
            dimension_semantics=("arbitrary",),
            vmem_limit_bytes=VMEM_LIMIT_BYTES,
        ),
        name="layer",
    )(*operands)
```

```python
import functools

import jax
import jax.numpy as jnp
from jax.experimental import pallas as pl
from jax.experimental.pallas import tpu as pltpu

EPS = 1e-6
HEAD_DIM = 64
SHORT_K = 3
CFM_K = 31

V7X_SUBLANES = 8
V7X_LANES = 128

TIME_TILE = 64
CONV_TIME = 8
FFN_CHUNK = 256
ADA_COLS = 1024
VMEM_LIMIT_BYTES = 56 * 1024 * 1024


def _unit_rms(v):
    return v * jax.lax.rsqrt(jnp.mean(v * v, axis=-1, keepdims=True) + EPS)


def _sigmoid(v):
    return 1.0 / (1.0 + jnp.exp(-v))


def _adaln_kernel(c_ref, w_ref, b_ref, o_ref):
    c = c_ref[...]
    c_act = (c * _sigmoid(c)).astype(jnp.bfloat16)
    acc = jnp.dot(c_act, w_ref[...].astype(jnp.bfloat16), preferred_element_type=jnp.float32)
    o_ref[...] = acc + b_ref[...]


def _per_batch_affine(v, scale, shift=None):
    nb = scale.shape[0]
    v3 = v.reshape(v.shape[0] // nb, nb, v.shape[1]) * scale[None]
    if shift is not None:
        v3 = v3 + shift[None]
    return v3.reshape(v.shape)


def _causal_dwconv(src_buf, w_ref, b_ref, dst_buf, n_taps, nb):
    rows = dst_buf.shape[0]
    width = dst_buf.shape[1]
    blk = CONV_TIME * nb
    halo = nb * (n_taps - 1)

    def row_block(i, carry):
        r0 = pl.multiple_of(i * blk, blk)
        for c0 in range(0, width, V7X_LANES):
            win = src_buf[pl.ds(r0, blk + halo), c0:c0 + V7X_LANES]
            acc = jnp.broadcast_to(b_ref[:, c0:c0 + V7X_LANES], (blk, V7X_LANES))
            for k in range(n_taps):
                acc = acc + w_ref[k:k + 1, c0:c0 + V7X_LANES] * win[nb * k:nb * k + blk]
            dst_buf[pl.ds(r0, blk), c0:c0 + V7X_LANES] = acc
        return carry

    jax.lax.fori_loop(0, rows // blk, row_block, 0)


def _tile_copies(hbm_ref, buf, sem, tile, slot, to_vmem):
    nt, nb = buf.shape[1], buf.shape[2]
    copies = []
    for b in range(nb):
        hbm = hbm_ref.at[b, pl.ds(tile * nt, nt), :]
        vmem = buf.at[slot, :, b, :]
        src, dst = (hbm, vmem) if to_vmem else (vmem, hbm)
        copies.append(pltpu.make_async_copy(src, dst, sem.at[slot]))
    return copies


def _layer_kernel(x_hbm, mod_ref, g_pre_mix_ref, g_post_mix_ref, w_in_ref, w_short_ref, b_short_ref,
                  w_cfm_ref, b_cfm_ref, g_ln_ref, b_ln_ref, beta_ref, seg_sum_ref, seg_bcast_ref, w_out_ref,
                  g_pre_ffn_ref, g_post_ffn_ref, w_gu_ref, w_down_ref,
                  o_hbm,
                  x_buf, o_buf, in_sem, out_sem, p_buf, u_buf, conv_s, conv_c, f_acc,
                  *, conv_w, ffn_hidden, nb, n_steps):
    step = pl.program_id(0)
    slot = jax.lax.rem(step, 2)
    nt, D = x_buf.shape[1], x_buf.shape[3]
    T = nt * nb
    bf16 = jnp.bfloat16
    f32 = jnp.float32
    p_halo = nb * (SHORT_K - 1)
    u_halo = nb * (CFM_K - 1)

    @pl.when(step == 0)
    def _():
        for cp in _tile_copies(x_hbm, x_buf, in_sem, step, slot, True):
            cp.start()

    @pl.when(step + 1 < n_steps)
    def _():
        for cp in _tile_copies(x_hbm, x_buf, in_sem, step + 1, 1 - slot, True):
            cp.start()

    for cp in _tile_copies(x_hbm, x_buf, in_sem, step, slot, True):
        cp.wait()

    x = x_buf[slot].reshape(T, D)
    mod = mod_ref[...]
    sh1, sc1, gt1, sh2, sc2, gt2 = (mod[:, i * D:(i + 1) * D] for i in range(6))

    gain1 = g_pre_mix_ref[...] * (1.0 + sc1)
    gate1 = g_post_mix_ref[...] * gt1
    gain2 = g_pre_ffn_ref[...] * (1.0 + sc2)
    gate2 = g_post_ffn_ref[...] * gt2

    h = _per_batch_affine(_unit_rms(x), gain1, sh1)
    z = jnp.dot(h.astype(bf16), w_in_ref[...], preferred_element_type=f32)
    gb = z[:, 0 * conv_w:1 * conv_w]
    gc = z[:, 1 * conv_w:2 * conv_w]
    v = z[:, 2 * conv_w:3 * conv_w]
    a = z[:, 3 * conv_w:4 * conv_w]
    g = z[:, 4 * conv_w:5 * conv_w]

    @pl.when(step == 0)
    def _():
        p_buf[0:p_halo, :] = jnp.zeros((p_halo, conv_w), f32)
        u_buf[0:u_halo, :] = jnp.zeros((u_halo, conv_w), f32)

    @pl.when(step > 0)
    def _():
        p_buf[0:p_halo, :] = p_buf[T:T + p_halo, :]
        u_buf[0:u_halo, :] = u_buf[T:T + u_halo, :]

    p_buf[p_halo:p_halo + T, :] = gc * v
    u_buf[u_halo:u_halo + T, :] = a * _sigmoid(g)

    _causal_dwconv(p_buf, w_short_ref, b_short_ref, conv_s, SHORT_K, nb)
    _causal_dwconv(u_buf, w_cfm_ref, b_cfm_ref, conv_c, CFM_K, nb)

    y_short = gb * conv_s[...]
    uc = conv_c[...]
    mu = jnp.mean(uc, axis=-1, keepdims=True)
    d = uc - mu
    var = jnp.mean(d * d, axis=-1, keepdims=True)
    ln = d * jax.lax.rsqrt(var + EPS) * g_ln_ref[...] + b_ln_ref[...]
    y_cfm = ln * _sigmoid(ln)

    y = jnp.concatenate([y_short, y_cfm], axis=-1)
    ms = jnp.dot((y * y).astype(bf16), seg_sum_ref[...], preferred_element_type=f32)
    r = jax.lax.rsqrt(ms + EPS)
    r_hi = r.astype(bf16)
    r_lo = (r - r_hi.astype(f32)).astype(bf16)
    r_full = jnp.dot(jnp.concatenate([r_hi, r_lo], axis=-1), seg_bcast_ref[...], preferred_element_type=f32)
    yn = y * r_full * beta_ref[...]
    o = jnp.dot(yn.astype(bf16), w_out_ref[...], preferred_element_type=f32)
    x1 = x + _per_batch_affine(_unit_rms(o), gate1)

    h2 = _per_batch_affine(_unit_rms(x1), gain2, sh2).astype(bf16)
    for j in range(ffn_hidden // FFN_CHUNK):
        c0 = j * FFN_CHUNK
        gate = jnp.dot(h2, w_gu_ref[:, c0:c0 + FFN_CHUNK], preferred_element_type=f32)
        up = jnp.dot(h2, w_gu_ref[:, ffn_hidden + c0:ffn_hidden + c0 + FFN_CHUNK], preferred_element_type=f32)
        act = (gate * _sigmoid(gate) * up).astype(bf16)
        part = jnp.dot(act, w_down_ref[c0:c0 + FFN_CHUNK, :], preferred_element_type=f32)
        if j == 0:
            f_acc[...] = part
        else:
            f_acc[...] += part
    out = x1 + _per_batch_affine(_unit_rms(f_acc[...]), gate2)

    @pl.when(step >= 2)
    def _():
        for cp in _tile_copies(o_hbm, o_buf, out_sem, step - 2, slot, False):
            cp.wait()

    o_buf[slot] = out.reshape(nt, nb, D)
    for cp in _tile_copies(o_hbm, o_buf, out_sem, step, slot, False):
        cp.start()

    @pl.when(step == n_steps - 1)
    def _():
        if n_steps >= 2:
            for cp in _tile_copies(o_hbm, o_buf, out_sem, step - 1, 1 - slot, False):
                cp.wait()
        for cp in _tile_copies(o_hbm, o_buf, out_sem, step, slot, False):
            cp.wait()


def _const_spec(shape):
    zeros = (0,) * len(shape)
    return pl.BlockSpec(shape, lambda s: zeros, pipeline_mode=pl.Buffered(1))


def kernel(x, c, w_ada, b_ada, g_pre_mix, g_post_mix, w_in, w_short, b_short, w_cfm_dw, b_cfm_dw,
           g_cfm_ln, b_cfm_ln, beta_mix, w_out, g_pre_ffn, g_post_ffn, w_gate_up, w_down):
    B, S, D = x.shape
    assert w_ada.shape[0] == 1, "single-layer block"
    n_mod = w_ada.shape[2] // D
    conv_w = w_short.shape[2]
    ffn_hidden = w_down.shape[1]
    assert B == V7X_SUBLANES and n_mod == 6
    assert w_in.shape[2] == 5 * conv_w and 2 * conv_w == D
    assert S % TIME_TILE == 0 and TIME_TILE % CONV_TIME == 0
    assert ffn_hidden % FFN_CHUNK == 0 and (n_mod * D) % ADA_COLS == 0
    bf16 = jnp.bfloat16
    T = TIME_TILE * B

    mod = pl.pallas_call(
        _adaln_kernel,
        out_shape=jax.ShapeDtypeStruct((B, n_mod * D), jnp.float32),
        grid=(n_mod * D // ADA_COLS,),
        in_specs=[
            pl.BlockSpec((B, D), lambda j: (0, 0)),
            pl.BlockSpec((D, ADA_COLS), lambda j: (0, j)),
            pl.BlockSpec((1, ADA_COLS), lambda j: (0, j)),
        ],
        out_specs=pl.BlockSpec((B, ADA_COLS), lambda j: (0, j)),
        compiler_params=pltpu.CompilerParams(dimension_semantics=("arbitrary",)),
        name="adaln",
    )(c, w_ada[0], b_ada)

    head = jnp.arange(D, dtype=jnp.int32) // HEAD_DIM
    member = head[:, None] == jnp.arange(V7X_LANES, dtype=jnp.int32)[None, :]
    seg_sum = jnp.where(member, 1.0 / HEAD_DIM, 0.0).astype(bf16)
    seg_bcast = jnp.tile(jnp.where(member.T, 1.0, 0.0), (2, 1)).astype(bf16)

    operands = (
        x, mod, g_pre_mix, g_post_mix, w_in[0].astype(bf16), w_short[0], b_short,
        w_cfm_dw[0], b_cfm_dw, g_cfm_ln, b_cfm_ln, beta_mix, seg_sum, seg_bcast, w_out[0].astype(bf16),
        g_pre_ffn, g_post_ffn, w_gate_up[0].astype(bf16), w_down[0].astype(bf16),
    )
    in_specs = [pl.BlockSpec(memory_space=pl.ANY)] + [_const_spec(t.shape) for t in operands[1:]]
    n_steps = S // TIME_TILE

    return pl.pallas_call(
        functools.partial(_layer_kernel, conv_w=conv_w, ffn_hidden=ffn_hidden, nb=B, n_steps=n_steps),
        out_shape=jax.ShapeDtypeStruct((B, S, D), x.dtype),
        grid=(n_steps,),
        in_specs=in_specs,
        out_specs=pl.BlockSpec(memory_space=pl.ANY),
        scratch_shapes=[
            pltpu.VMEM((2, TIME_TILE, B, D), jnp.float32),
            pltpu.VMEM((2, TIME_TILE, B, D), jnp.float32),
            pltpu.SemaphoreType.DMA((2,)),
            pltpu.SemaphoreType.DMA((2,)),
            pltpu.VMEM((B * (SHORT_K - 1) + T, conv_w), jnp.float32),
            pltpu.VMEM((B * (CFM_K - 1) + T, conv_w), jnp.float32),
            pltpu.VMEM((T, conv_w), jnp.float32),
            pltpu.VMEM((T, conv_w), jnp.float32),
            pltpu.VMEM((T, D), jnp.float32),
        ],
        compiler_params=pltpu.CompilerParams(
            dimension_semantics=("arbitrary",),
            vmem_limit_bytes=VMEM_LIMIT_BYTES,
        ),
        name="layer",
    )(*operands)
```

```python
import functools

import jax
import jax.numpy as jnp
from jax.experimental import pallas as pl
from jax.experimental.pallas import tpu as pltpu

EPS = 1e-6
HEAD_DIM = 64
SHORT_K = 3
CFM_K = 31

V7X_SUBLANES = 8
V7X_LANES = 128

TIME_TILE = 64
CONV_TIME = 8
FFN_CHUNK = 256
X_SLOTS = 3
PRE_NORM_PIECES = 4
ADA_COLS = 1024
VMEM_LIMIT_BYTES = 56 * 1024 * 1024


def _unit_rms(v):
    return v * jax.lax.rsqrt(jnp.mean(v * v, axis=-1, keepdims=True) + EPS)


def _sigmoid(v):
    return 1.0 / (1.0 + jnp.exp(-v))


def _adaln_kernel(c_ref, w_ref, b_ref, o_ref):
    c = c_ref[...]
    c_act = (c * _sigmoid(c)).astype(jnp.bfloat16)
    acc = jnp.dot(c_act, w_ref[...].astype(jnp.bfloat16), preferred_element_type=jnp.float32)
    o_ref[...] = acc + b_ref[...]


def _per_batch_affine(v, scale, shift=None):
    nb = scale.shape[0]
    v3 = v.reshape(v.shape[0] // nb, nb, v.shape[1]) * scale[None]
    if shift is not None:
        v3 = v3 + shift[None]
    return v3.reshape(v.shape)


def _causal_dwconv(src_buf, w_ref, b_ref, dst_buf, n_taps, nb):
    rows = dst_buf.shape[0]
    width = dst_buf.shape[1]
    blk = CONV_TIME * nb
    halo = nb * (n_taps - 1)

    def row_block(i, carry):
        r0 = pl.multiple_of(i * blk, blk)
        for c0 in range(0, width, V7X_LANES):
            win = src_buf[pl.ds(r0, blk + halo), c0:c0 + V7X_LANES]
            acc = jnp.broadcast_to(b_ref[:, c0:c0 + V7X_LANES], (blk, V7X_LANES))
            for k in range(n_taps):
                acc = acc + w_ref[k:k + 1, c0:c0 + V7X_LANES] * win[nb * k:nb * k + blk]
            dst_buf[pl.ds(r0, blk), c0:c0 + V7X_LANES] = acc
        return carry

    jax.lax.fori_loop(0, rows // blk, row_block, 0)


def _tile_copies(hbm_ref, buf, sem, tile, slot, to_vmem):
    nt, nb = buf.shape[1], buf.shape[2]
    copies = []
    for b in range(nb):
        hbm = hbm_ref.at[b, pl.ds(tile * nt, nt), :]
        vmem = buf.at[slot, :, b, :]
        src, dst = (hbm, vmem) if to_vmem else (vmem, hbm)
        copies.append(pltpu.make_async_copy(src, dst, sem.at[slot]))
    return copies


def _layer_kernel(x_hbm, mod_ref, g_pre_mix_ref, g_post_mix_ref, w_in_ref, w_short_ref, b_short_ref,
                  w_cfm_ref, b_cfm_ref, g_ln_ref, b_ln_ref, beta_ref, seg_sum_ref, seg_bcast_ref, w_out_ref,
                  g_pre_ffn_ref, g_post_ffn_ref, w_gu_ref, w_down_ref,
                  o_hbm,
                  x_buf, o_buf, in_sem, out_sem, p_buf, u_buf, conv_s, conv_c, f_acc, x1_buf, hb_buf,
                  *, conv_w, ffn_hidden, nb, n_tiles):
    step = pl.program_id(0)
    x_slot = jax.lax.rem(step, X_SLOTS)
    x_next = jax.lax.rem(step + 1, X_SLOTS)
    o_slot = jax.lax.rem(step + 1, 2)
    nt, D = x_buf.shape[1], x_buf.shape[3]
    T = nt * nb
    bf16 = jnp.bfloat16
    f32 = jnp.float32
    p_halo = nb * (SHORT_K - 1)
    u_halo = nb * (CFM_K - 1)

    mod = mod_ref[...]
    sh1, sc1, gt1, sh2, sc2, gt2 = (mod[:, i * D:(i + 1) * D] for i in range(6))
    gain1 = g_pre_mix_ref[...] * (1.0 + sc1)
    gate1 = g_post_mix_ref[...] * gt1
    gain2 = g_pre_ffn_ref[...] * (1.0 + sc2)
    gate2 = g_post_ffn_ref[...] * gt2

    def pre_norm(slot, rows):
        xt = x_buf[slot, rows].reshape(-1, D)
        return _per_batch_affine(_unit_rms(xt), gain1, sh1).astype(bf16)

    def finish_previous(rows):
        r = slice(rows.start * nb, rows.stop * nb)
        out = x1_buf[r, :] + _per_batch_affine(_unit_rms(f_acc[r, :]), gate2)
        o_buf[o_slot, rows] = out.reshape(rows.stop - rows.start, nb, D)

    @pl.when(step == 0)
    def _():
        for t in range(min(2, n_tiles)):
            for cp in _tile_copies(x_hbm, x_buf, in_sem, t, t, True):
                cp.start()
        for cp in _tile_copies(x_hbm, x_buf, in_sem, 0, 0, True):
            cp.wait()
        hb_buf[...] = pre_norm(0, slice(0, nt))
        f_acc[...] = jnp.zeros(f_acc.shape, f32)
        x1_buf[...] = jnp.zeros(x1_buf.shape, f32)
        p_buf[0:p_halo, :] = jnp.zeros((p_halo, conv_w), f32)
        u_buf[0:u_halo, :] = jnp.zeros((u_halo, conv_w), f32)

    @pl.when(step + 2 < n_tiles)
    def _():
        for cp in _tile_copies(x_hbm, x_buf, in_sem, step + 2, jax.lax.rem(step + 2, X_SLOTS), True):
            cp.start()

    @pl.when(step + 1 < n_tiles)
    def _():
        for cp in _tile_copies(x_hbm, x_buf, in_sem, step + 1, x_next, True):
            cp.wait()

    @pl.when(step >= 3)
    def _():
        for cp in _tile_copies(o_hbm, o_buf, out_sem, step - 3, o_slot, False):
            cp.wait()

    @pl.when(jnp.logical_and(step > 0, step < n_tiles))
    def _():
        p_buf[0:p_halo, :] = p_buf[T:T + p_halo, :]
        u_buf[0:u_halo, :] = u_buf[T:T + u_halo, :]

    @pl.when(step < n_tiles)
    def _():
        hb = hb_buf[...]
        half = nt // 2
        z = jnp.dot(hb, w_in_ref[:, 3 * conv_w:5 * conv_w], preferred_element_type=f32)
        u_buf[u_halo:u_halo + T, :] = z[:, :conv_w] * _sigmoid(z[:, conv_w:])
        finish_previous(slice(0, half))
        z = jnp.dot(hb, w_in_ref[:, 1 * conv_w:3 * conv_w], preferred_element_type=f32)
        p_buf[p_halo:p_halo + T, :] = z[:, :conv_w] * z[:, conv_w:]
        finish_previous(slice(half, nt))
        gb = jnp.dot(hb, w_in_ref[:, 0:conv_w], preferred_element_type=f32)

        _causal_dwconv(p_buf, w_short_ref, b_short_ref, conv_s, SHORT_K, nb)
        _causal_dwconv(u_buf, w_cfm_ref, b_cfm_ref, conv_c, CFM_K, nb)

        y_short = gb * conv_s[...]
        uc = conv_c[...]
        mu = jnp.mean(uc, axis=-1, keepdims=True)
        d = uc - mu
        var = jnp.mean(d * d, axis=-1, keepdims=True)
        ln = d * jax.lax.rsqrt(var + EPS) * g_ln_ref[...] + b_ln_ref[...]
        y_cfm = ln * _sigmoid(ln)

        y = jnp.concatenate([y_short, y_cfm], axis=-1)
        ms = jnp.dot((y * y).astype(bf16), seg_sum_ref[...], preferred_element_type=f32)
        r = jax.lax.rsqrt(ms + EPS)
        r_hi = r.astype(bf16)
        r_lo = (r - r_hi.astype(f32)).astype(bf16)
        r_full = jnp.dot(jnp.concatenate([r_hi, r_lo], axis=-1), seg_bcast_ref[...], preferred_element_type=f32)
        yn = y * r_full * beta_ref[...]
        o = jnp.dot(yn.astype(bf16), w_out_ref[...], preferred_element_type=f32)
        x1 = x_buf[x_slot].reshape(T, D) + _per_batch_affine(_unit_rms(o), gate1)
        x1_buf[...] = x1

        h2 = _per_batch_affine(_unit_rms(x1), gain2, sh2).astype(bf16)
        n_chunks = ffn_hidden // FFN_CHUNK
        pre_rows = nt // PRE_NORM_PIECES
        for j in range(n_chunks):
            c0 = j * FFN_CHUNK
            gate = jnp.dot(h2, w_gu_ref[:, c0:c0 + FFN_CHUNK], preferred_element_type=f32)
            up = jnp.dot(h2, w_gu_ref[:, ffn_hidden + c0:ffn_hidden + c0 + FFN_CHUNK], preferred_element_type=f32)
            act = (gate * _sigmoid(gate) * up).astype(bf16)
            part = jnp.dot(act, w_down_ref[c0:c0 + FFN_CHUNK, :], preferred_element_type=f32)
            if j == 0:
                f_acc[...] = part
            else:
                f_acc[...] += part
            if 1 <= j <= PRE_NORM_PIECES:
                q = j - 1
                hb_buf[q * pre_rows * nb:(q + 1) * pre_rows * nb, :] = pre_norm(
                    x_next, slice(q * pre_rows, (q + 1) * pre_rows))

    @pl.when(step == n_tiles)
    def _():
        finish_previous(slice(0, nt))

    @pl.when(step >= 1)
    def _():
        for cp in _tile_copies(o_hbm, o_buf, out_sem, step - 1, o_slot, False):
            cp.start()

    @pl.when(step == n_tiles)
    def _():
        if n_tiles >= 2:
            for cp in _tile_copies(o_hbm, o_buf, out_sem, step - 2, 1 - o_slot, False):
                cp.wait()
        for cp in _tile_copies(o_hbm, o_buf, out_sem, step - 1, o_slot, False):
            cp.wait()


def _const_spec(shape):
    zeros = (0,) * len(shape)
    return pl.BlockSpec(shape, lambda s: zeros, pipeline_mode=pl.Buffered(1))


def kernel(x, c, w_ada, b_ada, g_pre_mix, g_post_mix, w_in, w_short, b_short, w_cfm_dw, b_cfm_dw,
           g_cfm_ln, b_cfm_ln, beta_mix, w_out, g_pre_ffn, g_post_ffn, w_gate_up, w_down):
    B, S, D = x.shape
    assert w_ada.shape[0] == 1, "single-layer block"
    n_mod = w_ada.shape[2] // D
    conv_w = w_short.shape[2]
    ffn_hidden = w_down.shape[1]
    assert B == V7X_SUBLANES and n_mod == 6
    assert w_in.shape[2] == 5 * conv_w and 2 * conv_w == D
    assert S % TIME_TILE == 0 and TIME_TILE % CONV_TIME == 0 and TIME_TILE % (2 * PRE_NORM_PIECES) == 0
    assert ffn_hidden % FFN_CHUNK == 0 and ffn_hidden // FFN_CHUNK > PRE_NORM_PIECES
    assert (n_mod * D) % ADA_COLS == 0
    bf16 = jnp.bfloat16
    T = TIME_TILE * B

    mod = pl.pallas_call(
        _adaln_kernel,
        out_shape=jax.ShapeDtypeStruct((B, n_mod * D), jnp.float32),
        grid=(n_mod * D // ADA_COLS,),
        in_specs=[
            pl.BlockSpec((B, D), lambda j: (0, 0)),
            pl.BlockSpec((D, ADA_COLS), lambda j: (0, j)),
            pl.BlockSpec((1, ADA_COLS), lambda j: (0, j)),
        ],
        out_specs=pl.BlockSpec((B, ADA_COLS), lambda j: (0, j)),
        compiler_params=pltpu.CompilerParams(dimension_semantics=("arbitrary",)),
        name="adaln",
    )(c, w_ada[0], b_ada)

    head = jnp.arange(D, dtype=jnp.int32) // HEAD_DIM
    member = head[:, None] == jnp.arange(V7X_LANES, dtype=jnp.int32)[None, :]
    seg_sum = jnp.where(member, 1.0 / HEAD_DIM, 0.0).astype(bf16)
    seg_bcast = jnp.tile(jnp.where(member.T, 1.0, 0.0), (2, 1)).astype(bf16)

    operands = (
        x, mod, g_pre_mix, g_post_mix, w_in[0].astype(bf16), w_short[0], b_short,
        w_cfm_dw[0], b_cfm_dw, g_cfm_ln, b_cfm_ln, beta_mix, seg_sum, seg_bcast, w_out[0].astype(bf16),
        g_pre_ffn, g_post_ffn, w_gate_up[0].astype(bf16), w_down[0].astype(bf16),
    )
    in_specs = [pl.BlockSpec(memory_space=pl.ANY)] + [_const_spec(t.shape) for t in operands[1:]]
    n_tiles = S // TIME_TILE

    return pl.pallas_call(
        functools.partial(_layer_kernel, conv_w=conv_w, ffn_hidden=ffn_hidden, nb=B, n_tiles=n_tiles),
        out_shape=jax.ShapeDtypeStruct((B, S, D), x.dtype),
        grid=(n_tiles + 1,),
        in_specs=in_specs,
        out_specs=pl.BlockSpec(memory_space=pl.ANY),
        scratch_shapes=[
            pltpu.VMEM((X_SLOTS, TIME_TILE, B, D), jnp.float32),
            pltpu.VMEM((2, TIME_TILE, B, D), jnp.float32),
            pltpu.SemaphoreType.DMA((X_SLOTS,)),
            pltpu.SemaphoreType.DMA((2,)),
            pltpu.VMEM((B * (SHORT_K - 1) + T, conv_w), jnp.float32),
            pltpu.VMEM((B * (CFM_K - 1) + T, conv_w), jnp.float32),
            pltpu.VMEM((T, conv_w), jnp.float32),
            pltpu.VMEM((T, conv_w), jnp.float32),
            pltpu.VMEM((T, D), jnp.float32),
            pltpu.VMEM((T, D), jnp.float32),
            pltpu.VMEM((T, D), jnp.bfloat16),
        ],
        compiler_params=pltpu.CompilerParams(
            dimension_semantics=("arbitrary",),
            vmem_limit_bytes=VMEM_LIMIT_BYTES,
        ),
        name="layer",
    )(*operands)
```

```python
import functools

import jax
import jax.numpy as jnp
from jax.experimental import pallas as pl
from jax.experimental.pallas import tpu as pltpu

EPS = 1e-6
HEAD_DIM = 64
SHORT_K = 3
CFM_K = 31

V7X_SUBLANES = 8
V7X_LANES = 128

TIME_TILE = 64
CONV_TIME = 8
FFN_CHUNK = 256
X_SLOTS = 3
PRE_NORM_PIECES = 4
ADA_COLS = 1024
CAST_ROWS = 256
CAST_COLS = 512
CAST_SLOTS = 10
VMEM_LIMIT_BYTES = 56 * 1024 * 1024


def _unit_rms(v):
    return v * jax.lax.rsqrt(jnp.mean(v * v, axis=-1, keepdims=True) + EPS)


def _sigmoid(v):
    return 1.0 / (1.0 + jnp.exp(-v))


def _adaln_kernel(c_ref, w_ref, b_ref, o_ref):
    c = c_ref[...]
    c_act = (c * _sigmoid(c)).astype(jnp.bfloat16)
    acc = jnp.dot(c_act, w_ref[...].astype(jnp.bfloat16), preferred_element_type=jnp.float32)
    o_ref[...] = acc + b_ref[...]


def _per_batch_affine(v, scale, shift=None):
    nb = scale.shape[0]
    v3 = v.reshape(v.shape[0] // nb, nb, v.shape[1]) * scale[None]
    if shift is not None:
        v3 = v3 + shift[None]
    return v3.reshape(v.shape)


def _causal_dwconv(src_buf, w_ref, b_ref, dst_buf, n_taps, nb):
    rows = dst_buf.shape[0]
    width = dst_buf.shape[1]
    blk = CONV_TIME * nb
    halo = nb * (n_taps - 1)

    def row_block(i, carry):
        r0 = pl.multiple_of(i * blk, blk)
        for c0 in range(0, width, V7X_LANES):
            win = src_buf[pl.ds(r0, blk + halo), c0:c0 + V7X_LANES]
            acc = jnp.broadcast_to(b_ref[:, c0:c0 + V7X_LANES], (blk, V7X_LANES))
            for k in range(n_taps):
                acc = acc + w_ref[k:k + 1, c0:c0 + V7X_LANES] * win[nb * k:nb * k + blk]
            dst_buf[pl.ds(r0, blk), c0:c0 + V7X_LANES] = acc
        return carry

    jax.lax.fori_loop(0, rows // blk, row_block, 0)


def _tile_copies(hbm_ref, buf, sem, tile, slot, to_vmem):
    nt, nb = buf.shape[1], buf.shape[2]
    copies = []
    for b in range(nb):
        hbm = hbm_ref.at[b, pl.ds(tile * nt, nt), :]
        vmem = buf.at[slot, :, b, :]
        src, dst = (hbm, vmem) if to_vmem else (vmem, hbm)
        copies.append(pltpu.make_async_copy(src, dst, sem.at[slot]))
    return copies


def _cast_weight(w_hbm, w_vmem, stage, sem):
    rows, cols = w_vmem.shape
    n_c = cols // CAST_COLS
    n = (rows // CAST_ROWS) * n_c
    ahead = CAST_SLOTS - 1

    def chunk_copy(k):
        slot = k % CAST_SLOTS
        r0, c0 = (k // n_c) * CAST_ROWS, (k % n_c) * CAST_COLS
        if not isinstance(k, int):
            r0, c0 = pl.multiple_of(r0, CAST_ROWS), pl.multiple_of(c0, CAST_COLS)
        src = w_hbm.at[0, pl.ds(r0, CAST_ROWS), pl.ds(c0, CAST_COLS)]
        return pltpu.make_async_copy(src, stage.at[slot], sem.at[slot]), slot, r0, c0

    for k in range(min(ahead, n)):
        chunk_copy(k)[0].start()

    def body(k, carry):
        @pl.when(k + ahead < n)
        def _():
            chunk_copy(k + ahead)[0].start()

        cp, slot, r0, c0 = chunk_copy(k)
        cp.wait()
        w_vmem[pl.ds(r0, CAST_ROWS), pl.ds(c0, CAST_COLS)] = stage[slot].astype(jnp.bfloat16)
        return carry

    jax.lax.fori_loop(0, n, body, 0)


def _layer_kernel(x_hbm, mod_ref, g_pre_mix_ref, g_post_mix_ref, w_in_hbm, w_short_ref, b_short_ref,
                  w_cfm_ref, b_cfm_ref, g_ln_ref, b_ln_ref, beta_ref, seg_sum_ref, seg_bcast_ref, w_out_hbm,
                  g_pre_ffn_ref, g_post_ffn_ref, w_gu_hbm, w_down_hbm,
                  o_hbm,
                  x_buf, o_buf, in_sem, out_sem, p_buf, u_buf, conv_s, conv_c, f_acc, x1_buf, hb_buf,
                  w_in_ref, w_out_ref, w_gu_ref, w_down_ref, cast_stage, cast_sem,
                  *, conv_w, ffn_hidden, nb, n_tiles):
    step = pl.program_id(0)
    x_slot = jax.lax.rem(step, X_SLOTS)
    x_next = jax.lax.rem(step + 1, X_SLOTS)
    o_slot = jax.lax.rem(step + 1, 2)
    nt, D = x_buf.shape[1], x_buf.shape[3]
    T = nt * nb
    bf16 = jnp.bfloat16
    f32 = jnp.float32
    p_halo = nb * (SHORT_K - 1)
    u_halo = nb * (CFM_K - 1)

    mod = mod_ref[...]
    sh1, sc1, gt1, sh2, sc2, gt2 = (mod[:, i * D:(i + 1) * D] for i in range(6))
    gain1 = g_pre_mix_ref[...] * (1.0 + sc1)
    gate1 = g_post_mix_ref[...] * gt1
    gain2 = g_pre_ffn_ref[...] * (1.0 + sc2)
    gate2 = g_post_ffn_ref[...] * gt2

    def pre_norm(slot, rows):
        xt = x_buf[slot, rows].reshape(-1, D)
        return _per_batch_affine(_unit_rms(xt), gain1, sh1).astype(bf16)

    def finish_previous(rows):
        r = slice(rows.start * nb, rows.stop * nb)
        out = x1_buf[r, :] + _per_batch_affine(_unit_rms(f_acc[r, :]), gate2)
        o_buf[o_slot, rows] = out.reshape(rows.stop - rows.start, nb, D)

    @pl.when(step == 0)
    def _():
        for t in range(min(2, n_tiles)):
            for cp in _tile_copies(x_hbm, x_buf, in_sem, t, t, True):
                cp.start()
        for w_hbm, w_ref in ((w_in_hbm, w_in_ref), (w_out_hbm, w_out_ref), (w_gu_hbm, w_gu_ref),
                             (w_down_hbm, w_down_ref)):
            _cast_weight(w_hbm, w_ref, cast_stage, cast_sem)
        for cp in _tile_copies(x_hbm, x_buf, in_sem, 0, 0, True):
            cp.wait()
        hb_buf[...] = pre_norm(0, slice(0, nt))
        f_acc[...] = jnp.zeros(f_acc.shape, f32)
        x1_buf[...] = jnp.zeros(x1_buf.shape, f32)
        p_buf[0:p_halo, :] = jnp.zeros((p_halo, conv_w), f32)
        u_buf[0:u_halo, :] = jnp.zeros((u_halo, conv_w), f32)

    @pl.when(step + 2 < n_tiles)
    def _():
        for cp in _tile_copies(x_hbm, x_buf, in_sem, step + 2, jax.lax.rem(step + 2, X_SLOTS), True):
            cp.start()

    @pl.when(step + 1 < n_tiles)
    def _():
        for cp in _tile_copies(x_hbm, x_buf, in_sem, step + 1, x_next, True):
            cp.wait()

    @pl.when(step >= 3)
    def _():
        for cp in _tile_copies(o_hbm, o_buf, out_sem, step - 3, o_slot, False):
            cp.wait()

    @pl.when(jnp.logical_and(step > 0, step < n_tiles))
    def _():
        p_buf[0:p_halo, :] = p_buf[T:T + p_halo, :]
        u_buf[0:u_halo, :] = u_buf[T:T + u_halo, :]

    @pl.when(step < n_tiles)
    def _():
        hb = hb_buf[...]
        half = nt // 2
        z = jnp.dot(hb, w_in_ref[:, 3 * conv_w:5 * conv_w], preferred_element_type=f32)
        u_buf[u_halo:u_halo + T, :] = z[:, :conv_w] * _sigmoid(z[:, conv_w:])
        finish_previous(slice(0, half))
        z = jnp.dot(hb, w_in_ref[:, 1 * conv_w:3 * conv_w], preferred_element_type=f32)
        p_buf[p_halo:p_halo + T, :] = z[:, :conv_w] * z[:, conv_w:]
        finish_previous(slice(half, nt))
        gb = jnp.dot(hb, w_in_ref[:, 0:conv_w], preferred_element_type=f32)

        _causal_dwconv(p_buf, w_short_ref, b_short_ref, conv_s, SHORT_K, nb)
        _causal_dwconv(u_buf, w_cfm_ref, b_cfm_ref, conv_c, CFM_K, nb)

        y_short = gb * conv_s[...]
        uc = conv_c[...]
        mu = jnp.mean(uc, axis=-1, keepdims=True)
        d = uc - mu
        var = jnp.mean(d * d, axis=-1, keepdims=True)
        ln = d * jax.lax.rsqrt(var + EPS) * g_ln_ref[...] + b_ln_ref[...]
        y_cfm = ln * _sigmoid(ln)

        y = jnp.concatenate([y_short, y_cfm], axis=-1)
        ms = jnp.dot((y * y).astype(bf16), seg_sum_ref[...], preferred_element_type=f32)
        r = jax.lax.rsqrt(ms + EPS)
        r_hi = r.astype(bf16)
        r_lo = (r - r_hi.astype(f32)).astype(bf16)
        r_full = jnp.dot(jnp.concatenate([r_hi, r_lo], axis=-1), seg_bcast_ref[...], preferred_element_type=f32)
        yn = y * r_full * beta_ref[...]
        o = jnp.dot(yn.astype(bf16), w_out_ref[...], preferred_element_type=f32)
        x1 = x_buf[x_slot].reshape(T, D) + _per_batch_affine(_unit_rms(o), gate1)
        x1_buf[...] = x1

        h2 = _per_batch_affine(_unit_rms(x1), gain2, sh2).astype(bf16)
        n_chunks = ffn_hidden // FFN_CHUNK
        pre_rows = nt // PRE_NORM_PIECES
        for j in range(n_chunks):
            c0 = j * FFN_CHUNK
            gate = jnp.dot(h2, w_gu_ref[:, c0:c0 + FFN_CHUNK], preferred_element_type=f32)
            up = jnp.dot(h2, w_gu_ref[:, ffn_hidden + c0:ffn_hidden + c0 + FFN_CHUNK], preferred_element_type=f32)
            act = (gate * _sigmoid(gate) * up).astype(bf16)
            part = jnp.dot(act, w_down_ref[c0:c0 + FFN_CHUNK, :], preferred_element_type=f32)
            if j == 0:
                f_acc[...] = part
            else:
                f_acc[...] += part
            if 1 <= j <= PRE_NORM_PIECES:
                q = j - 1
                hb_buf[q * pre_rows * nb:(q + 1) * pre_rows * nb, :] = pre_norm(
                    x_next, slice(q * pre_rows, (q + 1) * pre_rows))

    @pl.when(step == n_tiles)
    def _():
        finish_previous(slice(0, nt))

    @pl.when(step >= 1)
    def _():
        for cp in _tile_copies(o_hbm, o_buf, out_sem, step - 1, o_slot, False):
            cp.start()

    @pl.when(step == n_tiles)
    def _():
        if n_tiles >= 2:
            for cp in _tile_copies(o_hbm, o_buf, out_sem, step - 2, 1 - o_slot, False):
                cp.wait()
        for cp in _tile_copies(o_hbm, o_buf, out_sem, step - 1, o_slot, False):
            cp.wait()


def _const_spec(shape):
    zeros = (0,) * len(shape)
    return pl.BlockSpec(shape, lambda s: zeros, pipeline_mode=pl.Buffered(1))


def kernel(x, c, w_ada, b_ada, g_pre_mix, g_post_mix, w_in, w_short, b_short, w_cfm_dw, b_cfm_dw,
           g_cfm_ln, b_cfm_ln, beta_mix, w_out, g_pre_ffn, g_post_ffn, w_gate_up, w_down):
    B, S, D = x.shape
    assert w_ada.shape[0] == 1, "single-layer block"
    n_mod = w_ada.shape[2] // D
    conv_w = w_short.shape[2]
    ffn_hidden = w_down.shape[1]
    assert B == V7X_SUBLANES and n_mod == 6
    assert w_in.shape[2] == 5 * conv_w and 2 * conv_w == D
    assert S % TIME_TILE == 0 and TIME_TILE % CONV_TIME == 0 and TIME_TILE % (2 * PRE_NORM_PIECES) == 0
    assert ffn_hidden % FFN_CHUNK == 0 and ffn_hidden // FFN_CHUNK > PRE_NORM_PIECES
    assert (n_mod * D) % ADA_COLS == 0
    bf16 = jnp.bfloat16
    T = TIME_TILE * B

    mod = pl.pallas_call(
        _adaln_kernel,
        out_shape=jax.ShapeDtypeStruct((B, n_mod * D), jnp.float32),
        grid=(n_mod * D // ADA_COLS,),
        in_specs=[
            pl.BlockSpec((B, D), lambda j: (0, 0)),
            pl.BlockSpec((D, ADA_COLS), lambda j: (0, j)),
            pl.BlockSpec((1, ADA_COLS), lambda j: (0, j)),
        ],
        out_specs=pl.BlockSpec((B, ADA_COLS), lambda j: (0, j)),
        compiler_params=pltpu.CompilerParams(dimension_semantics=("arbitrary",)),
        name="adaln",
    )(c, w_ada[0], b_ada)

    head = jnp.arange(D, dtype=jnp.int32) // HEAD_DIM
    member = head[:, None] == jnp.arange(V7X_LANES, dtype=jnp.int32)[None, :]
    seg_sum = jnp.where(member, 1.0 / HEAD_DIM, 0.0).astype(bf16)
    seg_bcast = jnp.tile(jnp.where(member.T, 1.0, 0.0), (2, 1)).astype(bf16)

    weights_hbm = (w_in, w_out, w_gate_up, w_down)
    for w in weights_hbm:
        assert w.shape[1] % CAST_ROWS == 0 and w.shape[2] % CAST_COLS == 0
    operands = (
        x, mod, g_pre_mix, g_post_mix, w_in, w_short[0], b_short,
        w_cfm_dw[0], b_cfm_dw, g_cfm_ln, b_cfm_ln, beta_mix, seg_sum, seg_bcast, w_out,
        g_pre_ffn, g_post_ffn, w_gate_up, w_down,
    )
    in_specs = [pl.BlockSpec(memory_space=pl.ANY) if any(t is h for h in (x,) + weights_hbm)
                else _const_spec(t.shape) for t in operands]
    n_tiles = S // TIME_TILE

    return pl.pallas_call(
        functools.partial(_layer_kernel, conv_w=conv_w, ffn_hidden=ffn_hidden, nb=B, n_tiles=n_tiles),
        out_shape=jax.ShapeDtypeStruct((B, S, D), x.dtype),
        grid=(n_tiles + 1,),
        in_specs=in_specs,
        out_specs=pl.BlockSpec(memory_space=pl.ANY),
        scratch_shapes=[
            pltpu.VMEM((X_SLOTS, TIME_TILE, B, D), jnp.float32),
            pltpu.VMEM((2, TIME_TILE, B, D), jnp.float32),
            pltpu.SemaphoreType.DMA((X_SLOTS,)),
            pltpu.SemaphoreType.DMA((2,)),
            pltpu.VMEM((B * (SHORT_K - 1) + T, conv_w), jnp.float32),
            pltpu.VMEM((B * (CFM_K - 1) + T, conv_w), jnp.float32),
            pltpu.VMEM((T, conv_w), jnp.float32),
            pltpu.VMEM((T, conv_w), jnp.float32),
            pltpu.VMEM((T, D), jnp.float32),
            pltpu.VMEM((T, D), jnp.float32),
            pltpu.VMEM((T, D), jnp.bfloat16),
            pltpu.VMEM(w_in.shape[1:], jnp.bfloat16),
            pltpu.VMEM(w_out.shape[1:], jnp.bfloat16),
            pltpu.VMEM(w_gate_up.shape[1:], jnp.bfloat16),
            pltpu.VMEM(w_down.shape[1:], jnp.bfloat16),
            pltpu.VMEM((CAST_SLOTS, CAST_ROWS, CAST_COLS), jnp.float32),
            pltpu.SemaphoreType.DMA((CAST_SLOTS,)),
        ],
        compiler_params=pltpu.CompilerParams(
            dimension_semantics=("arbitrary",),
            vmem_limit_bytes=VMEM_LIMIT_BYTES,
        ),
        name="layer",
    )(*operands)
```

```python
import functools

import jax
import jax.numpy as jnp
from jax.experimental import pallas as pl
from jax.experimental.pallas import tpu as pltpu

EPS = 1e-6
HEAD_DIM = 64
SHORT_K = 3
CFM_K = 31

V7X_SUBLANES = 8
V7X_LANES = 128

TIME_TILE = 64
CONV_TIME = 8
FFN_CHUNK = 256
X_SLOTS = 3
PRE_NORM_PIECES = 4
ADA_COLS = 2048
CAST_ROWS = 256
CAST_COLS = 512
CAST_SLOTS = 10
VMEM_LIMIT_BYTES = 56 * 1024 * 1024


def _unit_rms(v):
    return v * jax.lax.rsqrt(jnp.mean(v * v, axis=-1, keepdims=True) + EPS)


def _sigmoid(v):
    return 1.0 / (1.0 + jnp.exp(-v))


def _adaln_kernel(c_ref, w_ref, b_ref, o_ref):
    c = c_ref[...]
    c_act = (c * _sigmoid(c)).astype(jnp.bfloat16)
    acc = jnp.dot(c_act, w_ref[0].astype(jnp.bfloat16), preferred_element_type=jnp.float32)
    o_ref[...] = acc + b_ref[...]


def _per_batch_affine(v, scale, shift=None):
    nb = scale.shape[0]
    v3 = v.reshape(v.shape[0] // nb, nb, v.shape[1]) * scale[None]
    if shift is not None:
        v3 = v3 + shift[None]
    return v3.reshape(v.shape)


def _causal_dwconv(src_buf, w_ref, b_ref, dst_buf, n_taps, nb):
    rows = dst_buf.shape[0]
    width = dst_buf.shape[1]
    blk = CONV_TIME * nb
    halo = nb * (n_taps - 1)

    def row_block(i, carry):
        r0 = pl.multiple_of(i * blk, blk)
        for c0 in range(0, width, V7X_LANES):
            win = src_buf[pl.ds(r0, blk + halo), c0:c0 + V7X_LANES]
            acc = jnp.broadcast_to(b_ref[:, c0:c0 + V7X_LANES], (blk, V7X_LANES))
            for k in range(n_taps):
                acc = acc + w_ref[k:k + 1, c0:c0 + V7X_LANES] * win[nb * k:nb * k + blk]
            dst_buf[pl.ds(r0, blk), c0:c0 + V7X_LANES] = acc
        return carry

    jax.lax.fori_loop(0, rows // blk, row_block, 0)


def _fast_dwconv(uo_buf, d1_buf, d2_buf, w_ref, w_sum_ref, b_ref, dst_buf, n_taps, nb):
    n_e, n_o = (n_taps + 1) // 2, n_taps // 2
    rows, width = dst_buf.shape
    pairs = CONV_TIME
    blk = pairs * nb

    def corr(buf, r0, c0, taps, weight):
        win = buf[pl.ds(r0, blk + nb * (taps - 1)), c0:c0 + V7X_LANES]
        acc = weight(0) * win[0:blk]
        for m in range(1, taps):
            acc = acc + weight(m) * win[nb * m:nb * m + blk]
        return acc

    def pair_block(i, carry):
        r0 = pl.multiple_of(i * blk, blk)
        for c0 in range(0, width, V7X_LANES):
            lanes = slice(c0, c0 + V7X_LANES)
            shared = corr(uo_buf, r0, c0, n_e, lambda m: w_sum_ref[m:m + 1, lanes]) + b_ref[:, lanes]
            even = shared + corr(d1_buf, r0, c0, n_e, lambda m: w_ref[2 * m:2 * m + 1, lanes])
            odd = shared + corr(d2_buf, r0, c0, n_o, lambda m: w_ref[2 * m + 1:2 * m + 2, lanes])
            both = jnp.stack([even.reshape(pairs, nb, V7X_LANES), odd.reshape(pairs, nb, V7X_LANES)], axis=1)
            dst_buf[pl.ds(2 * r0, 2 * blk), lanes] = both.reshape(2 * blk, V7X_LANES)
        return carry

    jax.lax.fori_loop(0, rows // (2 * blk), pair_block, 0)


def _tile_copies(hbm_ref, buf, sem, tile, slot, to_vmem):
    nt, nb = buf.shape[1], buf.shape[2]
    copies = []
    for b in range(nb):
        hbm = hbm_ref.at[b, pl.ds(tile * nt, nt), :]
        vmem = buf.at[slot, :, b, :]
        src, dst = (hbm, vmem) if to_vmem else (vmem, hbm)
        copies.append(pltpu.make_async_copy(src, dst, sem.at[slot]))
    return copies


def _cast_weight(w_hbm, w_vmem, stage, sem):
    rows, cols = w_vmem.shape
    n_c = cols // CAST_COLS
    n = (rows // CAST_ROWS) * n_c
    ahead = CAST_SLOTS - 1

    def chunk_copy(k):
        slot = k % CAST_SLOTS
        r0, c0 = (k // n_c) * CAST_ROWS, (k % n_c) * CAST_COLS
        if not isinstance(k, int):
            r0, c0 = pl.multiple_of(r0, CAST_ROWS), pl.multiple_of(c0, CAST_COLS)
        src = w_hbm.at[0, pl.ds(r0, CAST_ROWS), pl.ds(c0, CAST_COLS)]
        return pltpu.make_async_copy(src, stage.at[slot], sem.at[slot]), slot, r0, c0

    for k in range(min(ahead, n)):
        chunk_copy(k)[0].start()

    def body(k, carry):
        @pl.when(k + ahead < n)
        def _():
            chunk_copy(k + ahead)[0].start()

        cp, slot, r0, c0 = chunk_copy(k)
        cp.wait()
        w_vmem[pl.ds(r0, CAST_ROWS), pl.ds(c0, CAST_COLS)] = stage[slot].astype(jnp.bfloat16)
        return carry

    jax.lax.fori_loop(0, n, body, 0)


def _layer_kernel(x_hbm, mod_ref, g_pre_mix_ref, g_post_mix_ref, w_in_hbm, w_short_ref, b_short_ref,
                  w_cfm_ref, b_cfm_ref, g_ln_ref, b_ln_ref, beta_ref, seg_sum_ref, seg_bcast_ref, w_out_hbm,
                  g_pre_ffn_ref, g_post_ffn_ref, w_gu_hbm, w_down_hbm,
                  o_hbm,
                  x_buf, o_buf, in_sem, out_sem, p_buf, ue_buf, uo_buf, d1_buf, d2_buf, w_sum_buf, conv_s, conv_c,
                  f_acc, x1_buf, hb_buf, w_in_ref, w_out_ref, w_gu_ref, w_down_ref, cast_stage, cast_sem,
                  *, conv_w, ffn_hidden, nb, n_tiles):
    step = pl.program_id(0)
    x_slot = jax.lax.rem(step, X_SLOTS)
    x_next = jax.lax.rem(step + 1, X_SLOTS)
    o_slot = jax.lax.rem(step + 1, 2)
    nt, D = x_buf.shape[1], x_buf.shape[3]
    T = nt * nb
    bf16 = jnp.bfloat16
    f32 = jnp.float32
    p_halo = nb * (SHORT_K - 1)
    s_halo = nb * (CFM_K - 1) // 2
    n_even_taps = (CFM_K + 1) // 2
    w_cfm = w_cfm_ref.at[0]

    mod = mod_ref[...]
    sh1, sc1, gt1, sh2, sc2, gt2 = (mod[:, i * D:(i + 1) * D] for i in range(6))
    gain1 = g_pre_mix_ref[...] * (1.0 + sc1)
    gate1 = g_post_mix_ref[...] * gt1
    gain2 = g_pre_ffn_ref[...] * (1.0 + sc2)
    gate2 = g_post_ffn_ref[...] * gt2

    def pre_norm(slot, rows):
        xt = x_buf[slot, rows].reshape(-1, D)
        return _per_batch_affine(_unit_rms(xt), gain1, sh1).astype(bf16)

    def finish_previous(rows):
        r = slice(rows.start * nb, rows.stop * nb)
        out = x1_buf[r, :] + _per_batch_affine(_unit_rms(f_acc[r, :]), gate2)
        o_buf[o_slot, rows] = out.reshape(rows.stop - rows.start, nb, D)

    @pl.when(step == 0)
    def _():
        for t in range(min(2, n_tiles)):
            for cp in _tile_copies(x_hbm, x_buf, in_sem, t, t, True):
                cp.start()
        for w_hbm, w_ref in ((w_in_hbm, w_in_ref), (w_out_hbm, w_out_ref), (w_gu_hbm, w_gu_ref),
                             (w_down_hbm, w_down_ref)):
            _cast_weight(w_hbm, w_ref, cast_stage, cast_sem)
        for cp in _tile_copies(x_hbm, x_buf, in_sem, 0, 0, True):
            cp.wait()
        hb_buf[...] = pre_norm(0, slice(0, nt))
        f_acc[...] = jnp.zeros(f_acc.shape, f32)
        x1_buf[...] = jnp.zeros(x1_buf.shape, f32)
        p_buf[0:p_halo, :] = jnp.zeros((p_halo, conv_w), f32)
        ue_buf[0:s_halo, :] = jnp.zeros((s_halo, conv_w), f32)
        uo_buf[0:s_halo, :] = jnp.zeros((s_halo, conv_w), f32)
        for m in range(n_even_taps - 1):
            w_sum_buf[m:m + 1, :] = w_cfm[2 * m:2 * m + 1, :] + w_cfm[2 * m + 1:2 * m + 2, :]
        w_sum_buf[n_even_taps - 1:n_even_taps, :] = w_cfm[CFM_K - 1:CFM_K, :]

    @pl.when(step + 2 < n_tiles)
    def _():
        for cp in _tile_copies(x_hbm, x_buf, in_sem, step + 2, jax.lax.rem(step + 2, X_SLOTS), True):
            cp.start()

    @pl.when(step + 1 < n_tiles)
    def _():
        for cp in _tile_copies(x_hbm, x_buf, in_sem, step + 1, x_next, True):
            cp.wait()

    @pl.when(step >= 3)
    def _():
        for cp in _tile_copies(o_hbm, o_buf, out_sem, step - 3, o_slot, False):
            cp.wait()

    @pl.when(jnp.logical_and(step > 0, step < n_tiles))
    def _():
        p_buf[0:p_halo, :] = p_buf[T:T + p_halo, :]
        ue_buf[0:s_halo, :] = ue_buf[T // 2:T // 2 + s_halo, :]
        uo_buf[0:s_halo, :] = uo_buf[T // 2:T // 2 + s_halo, :]

    @pl.when(step < n_tiles)
    def _():
        hb = hb_buf[...]
        half = nt // 2
        z = jnp.dot(hb, w_in_ref[:, 3 * conv_w:5 * conv_w], preferred_element_type=f32)
        u = (z[:, :conv_w] * _sigmoid(z[:, conv_w:])).reshape(nt // 2, 2, nb, conv_w)
        ue_buf[s_halo:s_halo + T // 2, :] = u[:, 0].reshape(T // 2, conv_w)
        uo_buf[s_halo:s_halo + T // 2, :] = u[:, 1].reshape(T // 2, conv_w)
        d1_buf[...] = ue_buf[...] - uo_buf[...]
        d2_buf[...] = ue_buf[nb:, :] - uo_buf[0:s_halo + T // 2 - nb, :]
        finish_previous(slice(0, half))
        z = jnp.dot(hb, w_in_ref[:, 1 * conv_w:3 * conv_w], preferred_element_type=f32)
        p_buf[p_halo:p_halo + T, :] = z[:, :conv_w] * z[:, conv_w:]
        finish_previous(slice(half, nt))
        gb = jnp.dot(hb, w_in_ref[:, 0:conv_w], preferred_element_type=f32)

        _causal_dwconv(p_buf, w_short_ref.at[0], b_short_ref, conv_s, SHORT_K, nb)
        _fast_dwconv(uo_buf, d1_buf, d2_buf, w_cfm, w_sum_buf, b_cfm_ref, conv_c, CFM_K, nb)

        y_short = gb * conv_s[...]
        uc = conv_c[...]
        mu = jnp.mean(uc, axis=-1, keepdims=True)
        d = uc - mu
        var = jnp.mean(d * d, axis=-1, keepdims=True)
        ln = d * jax.lax.rsqrt(var + EPS) * g_ln_ref[...] + b_ln_ref[...]
        y_cfm = ln * _sigmoid(ln)

        y = jnp.concatenate([y_short, y_cfm], axis=-1)
        ms = jnp.dot((y * y).astype(bf16), seg_sum_ref[...], preferred_element_type=f32)
        r = jax.lax.rsqrt(ms + EPS)
        r_hi = r.astype(bf16)
        r_lo = (r - r_hi.astype(f32)).astype(bf16)
        r_full = jnp.dot(jnp.concatenate([r_hi, r_lo], axis=-1), seg_bcast_ref[...], preferred_element_type=f32)
        yn = y * r_full * beta_ref[...]
        o = jnp.dot(yn.astype(bf16), w_out_ref[...], preferred_element_type=f32)
        x1 = x_buf[x_slot].reshape(T, D) + _per_batch_affine(_unit_rms(o), gate1)
        x1_buf[...] = x1

        h2 = _per_batch_affine(_unit_rms(x1), gain2, sh2).astype(bf16)
        n_chunks = ffn_hidden // FFN_CHUNK
        pre_rows = nt // PRE_NORM_PIECES
        for j in range(n_chunks):
            c0 = j * FFN_CHUNK
            gate = jnp.dot(h2, w_gu_ref[:, c0:c0 + FFN_CHUNK], preferred_element_type=f32)
            up = jnp.dot(h2, w_gu_ref[:, ffn_hidden + c0:ffn_hidden + c0 + FFN_CHUNK], preferred_element_type=f32)
            act = (gate * _sigmoid(gate) * up).astype(bf16)
            part = jnp.dot(act, w_down_ref[c0:c0 + FFN_CHUNK, :], preferred_element_type=f32)
            if j == 0:
                f_acc[...] = part
            else:
                f_acc[...] += part
            if 1 <= j <= PRE_NORM_PIECES:
                q = j - 1
                hb_buf[q * pre_rows * nb:(q + 1) * pre_rows * nb, :] = pre_norm(
                    x_next, slice(q * pre_rows, (q + 1) * pre_rows))

    @pl.when(step == n_tiles)
    def _():
        finish_previous(slice(0, nt))

    @pl.when(step >= 1)
    def _():
        for cp in _tile_copies(o_hbm, o_buf, out_sem, step - 1, o_slot, False):
            cp.start()

    @pl.when(step == n_tiles)
    def _():
        if n_tiles >= 2:
            for cp in _tile_copies(o_hbm, o_buf, out_sem, step - 2, 1 - o_slot, False):
                cp.wait()
        for cp in _tile_copies(o_hbm, o_buf, out_sem, step - 1, o_slot, False):
            cp.wait()


def _const_spec(shape):
    zeros = (0,) * len(shape)
    return pl.BlockSpec(shape, lambda s: zeros, pipeline_mode=pl.Buffered(1))


def kernel(x, c, w_ada, b_ada, g_pre_mix, g_post_mix, w_in, w_short, b_short, w_cfm_dw, b_cfm_dw,
           g_cfm_ln, b_cfm_ln, beta_mix, w_out, g_pre_ffn, g_post_ffn, w_gate_up, w_down):
    B, S, D = x.shape
    assert w_ada.shape[0] == 1, "single-layer block"
    n_mod = w_ada.shape[2] // D
    conv_w = w_short.shape[2]
    ffn_hidden = w_down.shape[1]
    assert B == V7X_SUBLANES and n_mod == 6
    assert w_in.shape[2] == 5 * conv_w and 2 * conv_w == D
    assert S % TIME_TILE == 0 and TIME_TILE % (2 * CONV_TIME) == 0 and TIME_TILE % (2 * PRE_NORM_PIECES) == 0
    assert CFM_K % 2 == 1
    assert ffn_hidden % FFN_CHUNK == 0 and ffn_hidden // FFN_CHUNK > PRE_NORM_PIECES
    assert (n_mod * D) % ADA_COLS == 0
    bf16 = jnp.bfloat16
    T = TIME_TILE * B

    mod = pl.pallas_call(
        _adaln_kernel,
        out_shape=jax.ShapeDtypeStruct((B, n_mod * D), jnp.float32),
        grid=(n_mod * D // ADA_COLS,),
        in_specs=[
            pl.BlockSpec((B, D), lambda j: (0, 0)),
            pl.BlockSpec((1, D, ADA_COLS), lambda j: (0, 0, j)),
            pl.BlockSpec((1, ADA_COLS), lambda j: (0, j)),
        ],
        out_specs=pl.BlockSpec((B, ADA_COLS), lambda j: (0, j)),
        compiler_params=pltpu.CompilerParams(dimension_semantics=("arbitrary",)),
        name="adaln",
    )(c, w_ada, b_ada)

    head = jnp.arange(D, dtype=jnp.int32) // HEAD_DIM
    member = head[:, None] == jnp.arange(V7X_LANES, dtype=jnp.int32)[None, :]
    seg_sum = jnp.where(member, 1.0 / HEAD_DIM, 0.0).astype(bf16)
    seg_bcast = jnp.tile(jnp.where(member.T, 1.0, 0.0), (2, 1)).astype(bf16)

    weights_hbm = (w_in, w_out, w_gate_up, w_down)
    for w in weights_hbm:
        assert w.shape[1] % CAST_ROWS == 0 and w.shape[2] % CAST_COLS == 0
    operands = (
        x, mod, g_pre_mix, g_post_mix, w_in, w_short, b_short,
        w_cfm_dw, b_cfm_dw, g_cfm_ln, b_cfm_ln, beta_mix, seg_sum, seg_bcast, w_out,
        g_pre_ffn, g_post_ffn, w_gate_up, w_down,
    )
    in_specs = [pl.BlockSpec(memory_space=pl.ANY) if any(t is h for h in (x,) + weights_hbm)
                else _const_spec(t.shape) for t in operands]
    n_tiles = S // TIME_TILE

    return pl.pallas_call(
        functools.partial(_layer_kernel, conv_w=conv_w, ffn_hidden=ffn_hidden, nb=B, n_tiles=n_tiles),
        out_shape=jax.ShapeDtypeStruct((B, S, D), x.dtype),
        grid=(n_tiles + 1,),
        in_specs=in_specs,
        out_specs=pl.BlockSpec(memory_space=pl.ANY),
        scratch_shapes=[
            pltpu.VMEM((X_SLOTS, TIME_TILE, B, D), jnp.float32),
            pltpu.VMEM((2, TIME_TILE, B, D), jnp.float32),
            pltpu.SemaphoreType.DMA((X_SLOTS,)),
            pltpu.SemaphoreType.DMA((2,)),
            pltpu.VMEM((B * (SHORT_K - 1) + T, conv_w), jnp.float32),
            pltpu.VMEM((B * (CFM_K - 1) // 2 + T // 2, conv_w), jnp.float32),
            pltpu.VMEM((B * (CFM_K - 1) // 2 + T // 2, conv_w), jnp.float32),
            pltpu.VMEM((B * (CFM_K - 1) // 2 + T // 2, conv_w), jnp.float32),
            pltpu.VMEM((B * (CFM_K - 1) // 2 + T // 2 - B, conv_w), jnp.float32),
            pltpu.VMEM(((CFM_K + 1) // 2, conv_w), jnp.float32),
            pltpu.VMEM((T, conv_w), jnp.float32),
            pltpu.VMEM((T, conv_w), jnp.float32),
            pltpu.VMEM((T, D), jnp.float32),
            pltpu.VMEM((T, D), jnp.float32),
            pltpu.VMEM((T, D), jnp.bfloat16),
            pltpu.VMEM(w_in.shape[1:], jnp.bfloat16),
            pltpu.VMEM(w_out.shape[1:], jnp.bfloat16),
            pltpu.VMEM(w_gate_up.shape[1:], jnp.bfloat16),
            pltpu.VMEM(w_down.shape[1:], jnp.bfloat16),
            pltpu.VMEM((CAST_SLOTS, CAST_ROWS, CAST_COLS), jnp.float32),
            pltpu.SemaphoreType.DMA((CAST_SLOTS,)),
        ],
        compiler_params=pltpu.CompilerParams(
            dimension_semantics=("arbitrary",),
            vmem_limit_bytes=VMEM_LIMIT_BYTES,
        ),
        name="layer",
    )(*operands)
```

```python
import functools

import jax
import jax.numpy as jnp
from jax.experimental import pallas as pl
from jax.experimental.pallas import tpu as pltpu

EPS = 1e-6
HEAD_DIM = 64
SHORT_K = 3
CFM_K = 31

V7X_SUBLANES = 8
V7X_LANES = 128

TIME_TILE = 64
CONV_TIME = 8
FFN_CHUNK = 256
X_SLOTS = 3
PRE_NORM_PIECES = 4
POST_SPLIT = 2
ADA_COLS = 2048
CAST_ROWS = 256
CAST_COLS = 512
CAST_SLOTS = 10
VMEM_LIMIT_BYTES = 56 * 1024 * 1024


def _unit_rms(v):
    return v * jax.lax.rsqrt(jnp.mean(v * v, axis=-1, keepdims=True) + EPS)


def _sigmoid(v):
    return 1.0 / (1.0 + jnp.exp(-v))


def _adaln_kernel(c_ref, w_ref, b_ref, o_ref):
    c = c_ref[...]
    c_act = (c * _sigmoid(c)).astype(jnp.bfloat16)
    acc = jnp.dot(c_act, w_ref[0].astype(jnp.bfloat16), preferred_element_type=jnp.float32)
    o_ref[...] = acc + b_ref[...]


def _per_batch_affine(v, scale, shift=None):
    nb = scale.shape[0]
    v3 = v.reshape(v.shape[0] // nb, nb, v.shape[1]) * scale[None]
    if shift is not None:
        v3 = v3 + shift[None]
    return v3.reshape(v.shape)


def _causal_dwconv(src_buf, w_ref, b_ref, dst_buf, n_taps, nb):
    rows = dst_buf.shape[0]
    width = dst_buf.shape[1]
    blk = CONV_TIME * nb
    halo = nb * (n_taps - 1)

    def row_block(i, carry):
        r0 = pl.multiple_of(i * blk, blk)
        for c0 in range(0, width, V7X_LANES):
            win = src_buf[pl.ds(r0, blk + halo), c0:c0 + V7X_LANES]
            acc = jnp.broadcast_to(b_ref[:, c0:c0 + V7X_LANES], (blk, V7X_LANES))
            for k in range(n_taps):
                acc = acc + w_ref[k:k + 1, c0:c0 + V7X_LANES] * win[nb * k:nb * k + blk]
            dst_buf[pl.ds(r0, blk), c0:c0 + V7X_LANES] = acc
        return carry

    jax.lax.fori_loop(0, rows // blk, row_block, 0)


def _fast_dwconv(uo_buf, d1_buf, d2_buf, w_ref, w_sum_ref, b_ref, dst_buf, n_taps, nb):
    n_e, n_o = (n_taps + 1) // 2, n_taps // 2
    rows, width = dst_buf.shape
    pairs = CONV_TIME
    blk = pairs * nb

    def corr(buf, r0, c0, taps, weight):
        win = buf[pl.ds(r0, blk + nb * (taps - 1)), c0:c0 + V7X_LANES]
        acc = weight(0) * win[0:blk]
        for m in range(1, taps):
            acc = acc + weight(m) * win[nb * m:nb * m + blk]
        return acc

    def pair_block(i, carry):
        r0 = pl.multiple_of(i * blk, blk)
        for c0 in range(0, width, V7X_LANES):
            lanes = slice(c0, c0 + V7X_LANES)
            shared = corr(uo_buf, r0, c0, n_e, lambda m: w_sum_ref[m:m + 1, lanes]) + b_ref[:, lanes]
            even = shared + corr(d1_buf, r0, c0, n_e, lambda m: w_ref[2 * m:2 * m + 1, lanes])
            odd = shared + corr(d2_buf, r0, c0, n_o, lambda m: w_ref[2 * m + 1:2 * m + 2, lanes])
            both = jnp.stack([even.reshape(pairs, nb, V7X_LANES), odd.reshape(pairs, nb, V7X_LANES)], axis=1)
            dst_buf[pl.ds(2 * r0, 2 * blk), lanes] = both.reshape(2 * blk, V7X_LANES)
        return carry

    jax.lax.fori_loop(0, rows // (2 * blk), pair_block, 0)


def _tile_copies(hbm_ref, buf, sem, tile, slot, to_vmem):
    nt, nb = buf.shape[1], buf.shape[2]
    copies = []
    for b in range(nb):
        hbm = hbm_ref.at[b, pl.ds(tile * nt, nt), :]
        vmem = buf.at[slot, :, b, :]
        src, dst = (hbm, vmem) if to_vmem else (vmem, hbm)
        copies.append(pltpu.make_async_copy(src, dst, sem.at[slot]))
    return copies


def _cast_weight(w_hbm, w_vmem, stage, sem):
    rows, cols = w_vmem.shape
    n_c = cols // CAST_COLS
    n = (rows // CAST_ROWS) * n_c
    ahead = CAST_SLOTS - 1

    def chunk_copy(k):
        slot = k % CAST_SLOTS
        r0, c0 = (k // n_c) * CAST_ROWS, (k % n_c) * CAST_COLS
        if not isinstance(k, int):
            r0, c0 = pl.multiple_of(r0, CAST_ROWS), pl.multiple_of(c0, CAST_COLS)
        src = w_hbm.at[0, pl.ds(r0, CAST_ROWS), pl.ds(c0, CAST_COLS)]
        return pltpu.make_async_copy(src, stage.at[slot], sem.at[slot]), slot, r0, c0

    for k in range(min(ahead, n)):
        chunk_copy(k)[0].start()

    def body(k, carry):
        @pl.when(k + ahead < n)
        def _():
            chunk_copy(k + ahead)[0].start()

        cp, slot, r0, c0 = chunk_copy(k)
        cp.wait()
        w_vmem[pl.ds(r0, CAST_ROWS), pl.ds(c0, CAST_COLS)] = stage[slot].astype(jnp.bfloat16)
        return carry

    jax.lax.fori_loop(0, n, body, 0)


def _layer_kernel(x_hbm, mod_ref, g_pre_mix_ref, g_post_mix_ref, w_in_hbm, w_short_ref, b_short_ref,
                  w_cfm_ref, b_cfm_ref, g_ln_ref, b_ln_ref, beta_ref, seg_sum_ref, seg_bcast_ref, w_out_hbm,
                  g_pre_ffn_ref, g_post_ffn_ref, w_gu_hbm, w_down_hbm,
                  o_hbm,
                  x_buf, o_buf, in_sem, out_sem, p_buf, ue_buf, uo_buf, d1_buf, d2_buf, w_sum_buf, conv_s, conv_c,
                  f_acc, x1_buf, hb_buf, w_in_ref, w_out_ref, w_gu_ref, w_down_ref, cast_stage, cast_sem,
                  *, conv_w, ffn_hidden, nb, n_tiles):
    step = pl.program_id(0)
    x_slot = jax.lax.rem(step, X_SLOTS)
    x_next = jax.lax.rem(step + 1, X_SLOTS)
    o_slot = jax.lax.rem(step + 1, 2)
    nt, D = x_buf.shape[1], x_buf.shape[3]
    T = nt * nb
    bf16 = jnp.bfloat16
    f32 = jnp.float32
    p_halo = nb * (SHORT_K - 1)
    s_halo = nb * (CFM_K - 1) // 2
    n_even_taps = (CFM_K + 1) // 2
    w_cfm = w_cfm_ref.at[0]

    mod = mod_ref[...]
    sh1, sc1, gt1, sh2, sc2, gt2 = (mod[:, i * D:(i + 1) * D] for i in range(6))
    gain1 = g_pre_mix_ref[...] * (1.0 + sc1)
    gate1 = g_post_mix_ref[...] * gt1
    gain2 = g_pre_ffn_ref[...] * (1.0 + sc2)
    gate2 = g_post_ffn_ref[...] * gt2

    def pre_norm(slot, rows):
        xt = x_buf[slot, rows].reshape(-1, D)
        return _per_batch_affine(_unit_rms(xt), gain1, sh1).astype(bf16)

    def finish_previous(rows):
        r = slice(rows.start * nb, rows.stop * nb)
        out = x1_buf[r, :] + _per_batch_affine(_unit_rms(f_acc[r, :]), gate2)
        o_buf[o_slot, rows] = out.reshape(rows.stop - rows.start, nb, D)

    @pl.when(step == 0)
    def _():
        for t in range(min(2, n_tiles)):
            for cp in _tile_copies(x_hbm, x_buf, in_sem, t, t, True):
                cp.start()
        for w_hbm, w_ref in ((w_in_hbm, w_in_ref), (w_out_hbm, w_out_ref), (w_gu_hbm, w_gu_ref),
                             (w_down_hbm, w_down_ref)):
            _cast_weight(w_hbm, w_ref, cast_stage, cast_sem)
        for cp in _tile_copies(x_hbm, x_buf, in_sem, 0, 0, True):
            cp.wait()
        hb_buf[...] = pre_norm(0, slice(0, nt))
        f_acc[...] = jnp.zeros(f_acc.shape, f32)
        x1_buf[...] = jnp.zeros(x1_buf.shape, f32)
        p_buf[0:p_halo, :] = jnp.zeros((p_halo, conv_w), f32)
        ue_buf[0:s_halo, :] = jnp.zeros((s_halo, conv_w), f32)
        uo_buf[0:s_halo, :] = jnp.zeros((s_halo, conv_w), f32)
        for m in range(n_even_taps - 1):
            w_sum_buf[m:m + 1, :] = w_cfm[2 * m:2 * m + 1, :] + w_cfm[2 * m + 1:2 * m + 2, :]
        w_sum_buf[n_even_taps - 1:n_even_taps, :] = w_cfm[CFM_K - 1:CFM_K, :]

    @pl.when(step + 2 < n_tiles)
    def _():
        for cp in _tile_copies(x_hbm, x_buf, in_sem, step + 2, jax.lax.rem(step + 2, X_SLOTS), True):
            cp.start()

    @pl.when(step + 1 < n_tiles)
    def _():
        for cp in _tile_copies(x_hbm, x_buf, in_sem, step + 1, x_next, True):
            cp.wait()

    @pl.when(step >= 3)
    def _():
        for cp in _tile_copies(o_hbm, o_buf, out_sem, step - 3, o_slot, False):
            cp.wait()

    @pl.when(jnp.logical_and(step > 0, step < n_tiles))
    def _():
        p_buf[0:p_halo, :] = p_buf[T:T + p_halo, :]
        ue_buf[0:s_halo, :] = ue_buf[T // 2:T // 2 + s_halo, :]
        uo_buf[0:s_halo, :] = uo_buf[T // 2:T // 2 + s_halo, :]

    @pl.when(step < n_tiles)
    def _():
        hb = hb_buf[...]
        half = nt // 2
        z = jnp.dot(hb, w_in_ref[:, 3 * conv_w:5 * conv_w], preferred_element_type=f32)
        u = (z[:, :conv_w] * _sigmoid(z[:, conv_w:])).reshape(nt // 2, 2, nb, conv_w)
        ue_buf[s_halo:s_halo + T // 2, :] = u[:, 0].reshape(T // 2, conv_w)
        uo_buf[s_halo:s_halo + T // 2, :] = u[:, 1].reshape(T // 2, conv_w)
        d1_buf[...] = ue_buf[...] - uo_buf[...]
        d2_buf[...] = ue_buf[nb:, :] - uo_buf[0:s_halo + T // 2 - nb, :]
        finish_previous(slice(0, half))
        z = jnp.dot(hb, w_in_ref[:, 1 * conv_w:3 * conv_w], preferred_element_type=f32)
        p_buf[p_halo:p_halo + T, :] = z[:, :conv_w] * z[:, conv_w:]
        finish_previous(slice(half, nt))
        gb = jnp.dot(hb, w_in_ref[:, 0:conv_w], preferred_element_type=f32)

        _causal_dwconv(p_buf, w_short_ref.at[0], b_short_ref, conv_s, SHORT_K, nb)
        _fast_dwconv(uo_buf, d1_buf, d2_buf, w_cfm, w_sum_buf, b_cfm_ref, conv_c, CFM_K, nb)

        groups = [slice(q * T // POST_SPLIT, (q + 1) * T // POST_SPLIT) for q in range(POST_SPLIT)]
        t_groups = [slice(q * nt // POST_SPLIT, (q + 1) * nt // POST_SPLIT) for q in range(POST_SPLIT)]

        def mixed_heads(r):
            uc = conv_c[r, :]
            mu = jnp.mean(uc, axis=-1, keepdims=True)
            d = uc - mu
            var = jnp.mean(d * d, axis=-1, keepdims=True)
            ln = d * jax.lax.rsqrt(var + EPS) * g_ln_ref[...] + b_ln_ref[...]
            return jnp.concatenate([gb[r] * conv_s[r, :], ln * _sigmoid(ln)], axis=-1)

        def head_scale(y):
            ms = jnp.dot((y * y).astype(bf16), seg_sum_ref[...], preferred_element_type=f32)
            r = jax.lax.rsqrt(ms + EPS)
            r_hi = r.astype(bf16)
            r_lo = (r - r_hi.astype(f32)).astype(bf16)
            return jnp.dot(jnp.concatenate([r_hi, r_lo], axis=-1), seg_bcast_ref[...], preferred_element_type=f32)

        def project(y, r_full):
            yn = y * r_full * beta_ref[...]
            return jnp.dot(yn.astype(bf16), w_out_ref[...], preferred_element_type=f32)

        def residual_and_norm(q, o):
            x1 = x_buf[x_slot, t_groups[q]].reshape(-1, D) + _per_batch_affine(_unit_rms(o), gate1)
            x1_buf[groups[q], :] = x1
            return _per_batch_affine(_unit_rms(x1), gain2, sh2).astype(bf16)

        ys, scales, outs, h2_parts = {}, {}, {}, {}
        for stage in range(POST_SPLIT + 3):
            for q in range(POST_SPLIT):
                k = stage - q
                if k == 0:
                    ys[q] = mixed_heads(groups[q])
                elif k == 1:
                    scales[q] = head_scale(ys[q])
                elif k == 2:
                    outs[q] = project(ys[q], scales[q])
                elif k == 3:
                    h2_parts[q] = residual_and_norm(q, outs[q])

        h2 = jnp.concatenate([h2_parts[q] for q in range(POST_SPLIT)], axis=0)
        n_chunks = ffn_hidden // FFN_CHUNK
        pre_rows = nt // PRE_NORM_PIECES
        for j in range(n_chunks):
            c0 = j * FFN_CHUNK
            gate = jnp.dot(h2, w_gu_ref[:, c0:c0 + FFN_CHUNK], preferred_element_type=f32)
            up = jnp.dot(h2, w_gu_ref[:, ffn_hidden + c0:ffn_hidden + c0 + FFN_CHUNK], preferred_element_type=f32)
            act = (gate * _sigmoid(gate) * up).astype(bf16)
            part = jnp.dot(act, w_down_ref[c0:c0 + FFN_CHUNK, :], preferred_element_type=f32)
            if j == 0:
                f_acc[...] = part
            else:
                f_acc[...] += part
            if j % 2 == 0 and 1 <= j // 2 <= PRE_NORM_PIECES:
                q = j // 2 - 1
                hb_buf[q * pre_rows * nb:(q + 1) * pre_rows * nb, :] = pre_norm(
                    x_next, slice(q * pre_rows, (q + 1) * pre_rows))

    @pl.when(step == n_tiles)
    def _():
        finish_previous(slice(0, nt))

    @pl.when(step >= 1)
    def _():
        for cp in _tile_copies(o_hbm, o_buf, out_sem, step - 1, o_slot, False):
            cp.start()

    @pl.when(step == n_tiles)
    def _():
        if n_tiles >= 2:
            for cp in _tile_copies(o_hbm, o_buf, out_sem, step - 2, 1 - o_slot, False):
                cp.wait()
        for cp in _tile_copies(o_hbm, o_buf, out_sem, step - 1, o_slot, False):
            cp.wait()


def _const_spec(shape):
    zeros = (0,) * len(shape)
    return pl.BlockSpec(shape, lambda s: zeros, pipeline_mode=pl.Buffered(1))


def kernel(x, c, w_ada, b_ada, g_pre_mix, g_post_mix, w_in, w_short, b_short, w_cfm_dw, b_cfm_dw,
           g_cfm_ln, b_cfm_ln, beta_mix, w_out, g_pre_ffn, g_post_ffn, w_gate_up, w_down):
    B, S, D = x.shape
    assert w_ada.shape[0] == 1, "single-layer block"
    n_mod = w_ada.shape[2] // D
    conv_w = w_short.shape[2]
    ffn_hidden = w_down.shape[1]
    assert B == V7X_SUBLANES and n_mod == 6
    assert w_in.shape[2] == 5 * conv_w and 2 * conv_w == D
    assert S % TIME_TILE == 0 and TIME_TILE % (2 * CONV_TIME) == 0 and TIME_TILE % (2 * PRE_NORM_PIECES) == 0
    assert CFM_K % 2 == 1
    assert ffn_hidden % FFN_CHUNK == 0 and ffn_hidden // FFN_CHUNK > PRE_NORM_PIECES
    assert (n_mod * D) % ADA_COLS == 0
    bf16 = jnp.bfloat16
    T = TIME_TILE * B

    mod = pl.pallas_call(
        _adaln_kernel,
        out_shape=jax.ShapeDtypeStruct((B, n_mod * D), jnp.float32),
        grid=(n_mod * D // ADA_COLS,),
        in_specs=[
            pl.BlockSpec((B, D), lambda j: (0, 0)),
            pl.BlockSpec((1, D, ADA_COLS), lambda j: (0, 0, j)),
            pl.BlockSpec((1, ADA_COLS), lambda j: (0, j)),
        ],
        out_specs=pl.BlockSpec((B, ADA_COLS), lambda j: (0, j)),
        compiler_params=pltpu.CompilerParams(dimension_semantics=("arbitrary",)),
        name="adaln",
    )(c, w_ada, b_ada)

    head = jnp.arange(D, dtype=jnp.int32) // HEAD_DIM
    member = head[:, None] == jnp.arange(V7X_LANES, dtype=jnp.int32)[None, :]
    seg_sum = jnp.where(member, 1.0 / HEAD_DIM, 0.0).astype(bf16)
    seg_bcast = jnp.tile(jnp.where(member.T, 1.0, 0.0), (2, 1)).astype(bf16)

    weights_hbm = (w_in, w_out, w_gate_up, w_down)
    for w in weights_hbm:
        assert w.shape[1] % CAST_ROWS == 0 and w.shape[2] % CAST_COLS == 0
    operands = (
        x, mod, g_pre_mix, g_post_mix, w_in, w_short, b_short,
        w_cfm_dw, b_cfm_dw, g_cfm_ln, b_cfm_ln, beta_mix, seg_sum, seg_bcast, w_out,
        g_pre_ffn, g_post_ffn, w_gate_up, w_down,
    )
    in_specs = [pl.BlockSpec(memory_space=pl.ANY) if any(t is h for h in (x,) + weights_hbm)
                else _const_spec(t.shape) for t in operands]
    n_tiles = S // TIME_TILE

    return pl.pallas_call(
        functools.partial(_layer_kernel, conv_w=conv_w, ffn_hidden=ffn_hidden, nb=B, n_tiles=n_tiles),
        out_shape=jax.ShapeDtypeStruct((B, S, D), x.dtype),
        grid=(n_tiles + 1,),
        in_specs=in_specs,
        out_specs=pl.BlockSpec(memory_space=pl.ANY),
        scratch_shapes=[
            pltpu.VMEM((X_SLOTS, TIME_TILE, B, D), jnp.float32),
            pltpu.VMEM((2, TIME_TILE, B, D), jnp.float32),
            pltpu.SemaphoreType.DMA((X_SLOTS,)),
            pltpu.SemaphoreType.DMA((2,)),
            pltpu.VMEM((B * (SHORT_K - 1) + T, conv_w), jnp.float32),
            pltpu.VMEM((B * (CFM_K - 1) // 2 + T // 2, conv_w), jnp.float32),
            pltpu.VMEM((B * (CFM_K - 1) // 2 + T // 2, conv_w), jnp.float32),
            pltpu.VMEM((B * (CFM_K - 1) // 2 + T // 2, conv_w), jnp.float32),
            pltpu.VMEM((B * (CFM_K - 1) // 2 + T // 2 - B, conv_w), jnp.float32),
            pltpu.VMEM(((CFM_K + 1) // 2, conv_w), jnp.float32),
            pltpu.VMEM((T, conv_w), jnp.float32),
            pltpu.VMEM((T, conv_w), jnp.float32),
            pltpu.VMEM((T, D), jnp.float32),
            pltpu.VMEM((T, D), jnp.float32),
            pltpu.VMEM((T, D), jnp.bfloat16),
            pltpu.VMEM(w_in.shape[1:], jnp.bfloat16),
            pltpu.VMEM(w_out.shape[1:], jnp.bfloat16),
            pltpu.VMEM(w_gate_up.shape[1:], jnp.bfloat16),
            pltpu.VMEM(w_down.shape[1:], jnp.bfloat16),
            pltpu.VMEM((CAST_SLOTS, CAST_ROWS, CAST_COLS), jnp.float32),
            pltpu.SemaphoreType.DMA((CAST_SLOTS,)),
        ],
        compiler_params=pltpu.CompilerParams(
            dimension_semantics=("arbitrary",),
            vmem_limit_bytes=VMEM_LIMIT_BYTES,
        ),
        name="layer",
    )(*operands)
```

```python
import functools

import jax
import jax.numpy as jnp
from jax.experimental import pallas as pl
from jax.experimental.pallas import tpu as pltpu

EPS = 1e-6
HEAD_DIM = 64
SHORT_K = 3
CFM_K = 31

V7X_SUBLANES = 8
V7X_LANES = 128

TIME_TILE = 64
CONV_TIME = 8
FFN_CHUNK = 256
X_SLOTS = 3
PRE_NORM_PIECES = 4
POST_SPLIT = 2
ADA_COLS = 2048
CAST_ROWS = 256
CAST_COLS = 512
CAST_SLOTS = 10
VMEM_LIMIT_BYTES = 56 * 1024 * 1024


def _unit_rms(v):
    return v * jax.lax.rsqrt(jnp.mean(v * v, axis=-1, keepdims=True) + EPS)


def _sigmoid(v):
    return 1.0 / (1.0 + jnp.exp(-v))


def _adaln_kernel(c_ref, w_ref, b_ref, o_ref):
    c = c_ref[...]
    c_act = (c * _sigmoid(c)).astype(jnp.bfloat16)
    acc = jnp.dot(c_act, w_ref[0].astype(jnp.bfloat16), preferred_element_type=jnp.float32)
    o_ref[...] = acc + b_ref[...]


def _per_batch_affine(v, scale, shift=None):
    nb = scale.shape[0]
    v3 = v.reshape(v.shape[0] // nb, nb, v.shape[1]) * scale[None]
    if shift is not None:
        v3 = v3 + shift[None]
    return v3.reshape(v.shape)


def _fast_dwconv(sh_buf, d1_buf, d2_buf, w_ref, dst_buf, n_taps, nb):
    n_e, n_o = (n_taps + 1) // 2, n_taps // 2
    rows, width = dst_buf.shape
    pairs = CONV_TIME
    blk = pairs * nb

    def corr(buf, r0, c0, taps, weight):
        win = buf[pl.ds(r0, blk + nb * (taps - 1)), c0:c0 + V7X_LANES]
        acc = weight(0) * win[0:blk]
        for m in range(1, taps):
            acc = acc + weight(m) * win[nb * m:nb * m + blk]
        return acc

    def pair_block(i, carry):
        r0 = pl.multiple_of(i * blk, blk)
        for c0 in range(0, width, V7X_LANES):
            lanes = slice(c0, c0 + V7X_LANES)
            shared = sh_buf[pl.ds(r0, blk), lanes]
            even = shared + corr(d1_buf, r0, c0, n_e, lambda m: w_ref[2 * m:2 * m + 1, lanes])
            odd = shared + corr(d2_buf, r0, c0, n_o, lambda m: w_ref[2 * m + 1:2 * m + 2, lanes])
            both = jnp.stack([even.reshape(pairs, nb, V7X_LANES), odd.reshape(pairs, nb, V7X_LANES)], axis=1)
            dst_buf[pl.ds(2 * r0, 2 * blk), lanes] = both.reshape(2 * blk, V7X_LANES)
        return carry

    jax.lax.fori_loop(0, rows // (2 * blk), pair_block, 0)


def _tile_copies(hbm_ref, buf, sem, tile, slot, to_vmem):
    nt, nb = buf.shape[1], buf.shape[2]
    copies = []
    for b in range(nb):
        hbm = hbm_ref.at[b, pl.ds(tile * nt, nt), :]
        vmem = buf.at[slot, :, b, :]
        src, dst = (hbm, vmem) if to_vmem else (vmem, hbm)
        copies.append(pltpu.make_async_copy(src, dst, sem.at[slot]))
    return copies


def _cast_weight(w_hbm, w_vmem, stage, sem):
    rows, cols = w_vmem.shape
    n_c = cols // CAST_COLS
    n = (rows // CAST_ROWS) * n_c
    ahead = CAST_SLOTS - 1

    def chunk_copy(k):
        slot = k % CAST_SLOTS
        r0, c0 = (k // n_c) * CAST_ROWS, (k % n_c) * CAST_COLS
        if not isinstance(k, int):
            r0, c0 = pl.multiple_of(r0, CAST_ROWS), pl.multiple_of(c0, CAST_COLS)
        src = w_hbm.at[0, pl.ds(r0, CAST_ROWS), pl.ds(c0, CAST_COLS)]
        return pltpu.make_async_copy(src, stage.at[slot], sem.at[slot]), slot, r0, c0

    for k in range(min(ahead, n)):
        chunk_copy(k)[0].start()

    def body(k, carry):
        @pl.when(k + ahead < n)
        def _():
            chunk_copy(k + ahead)[0].start()

        cp, slot, r0, c0 = chunk_copy(k)
        cp.wait()
        w_vmem[pl.ds(r0, CAST_ROWS), pl.ds(c0, CAST_COLS)] = stage[slot].astype(jnp.bfloat16)
        return carry

    jax.lax.fori_loop(0, n, body, 0)


def _layer_kernel(x_hbm, mod_ref, g_pre_mix_ref, g_post_mix_ref, w_in_hbm, w_short_ref, b_short_ref,
                  w_cfm_ref, b_cfm_ref, g_ln_ref, b_ln_ref, beta_ref, seg_sum_ref, seg_bcast_ref, w_out_hbm,
                  g_pre_ffn_ref, g_post_ffn_ref, w_gu_hbm, w_down_hbm,
                  o_hbm,
                  x_buf, o_buf, in_sem, out_sem, p_buf, ue_buf, uo_buf, d1_buf, d2_buf, sh_buf, w_sum_buf, conv_s, conv_c,
                  f_acc, x1_buf, hb_buf, w_in_ref, w_out_ref, w_gu_ref, w_down_ref, cast_stage, cast_sem,
                  *, conv_w, ffn_hidden, nb, n_tiles):
    step = pl.program_id(0)
    x_slot = jax.lax.rem(step, X_SLOTS)
    x_next = jax.lax.rem(step + 1, X_SLOTS)
    o_slot = jax.lax.rem(step + 1, 2)
    nt, D = x_buf.shape[1], x_buf.shape[3]
    T = nt * nb
    bf16 = jnp.bfloat16
    f32 = jnp.float32
    p_halo = nb * (SHORT_K - 1)
    s_halo = nb * (CFM_K - 1) // 2
    n_even_taps = (CFM_K + 1) // 2
    w_cfm = w_cfm_ref.at[0]

    mod = mod_ref[...]
    sh1, sc1, gt1, sh2, sc2, gt2 = (mod[:, i * D:(i + 1) * D] for i in range(6))
    gain1 = g_pre_mix_ref[...] * (1.0 + sc1)
    gate1 = g_post_mix_ref[...] * gt1
    gain2 = g_pre_ffn_ref[...] * (1.0 + sc2)
    gate2 = g_post_ffn_ref[...] * gt2

    def pre_norm(slot, rows):
        xt = x_buf[slot, rows].reshape(-1, D)
        return _per_batch_affine(_unit_rms(xt), gain1, sh1).astype(bf16)

    def finish_previous(rows):
        r = slice(rows.start * nb, rows.stop * nb)
        out = x1_buf[r, :] + _per_batch_affine(_unit_rms(f_acc[r, :]), gate2)
        o_buf[o_slot, rows] = out.reshape(rows.stop - rows.start, nb, D)

    @pl.when(step == 0)
    def _():
        for t in range(min(2, n_tiles)):
            for cp in _tile_copies(x_hbm, x_buf, in_sem, t, t, True):
                cp.start()
        for w_hbm, w_ref in ((w_in_hbm, w_in_ref), (w_out_hbm, w_out_ref), (w_gu_hbm, w_gu_ref),
                             (w_down_hbm, w_down_ref)):
            _cast_weight(w_hbm, w_ref, cast_stage, cast_sem)
        for cp in _tile_copies(x_hbm, x_buf, in_sem, 0, 0, True):
            cp.wait()
        hb_buf[...] = pre_norm(0, slice(0, nt))
        f_acc[...] = jnp.zeros(f_acc.shape, f32)
        x1_buf[...] = jnp.zeros(x1_buf.shape, f32)
        p_buf[0:p_halo, :] = jnp.zeros((p_halo, conv_w), f32)
        ue_buf[0:s_halo, :] = jnp.zeros((s_halo, conv_w), f32)
        uo_buf[0:s_halo, :] = jnp.zeros((s_halo, conv_w), f32)
        for m in range(n_even_taps - 1):
            w_sum_buf[m:m + 1, :] = w_cfm[2 * m:2 * m + 1, :] + w_cfm[2 * m + 1:2 * m + 2, :]
        w_sum_buf[n_even_taps - 1:n_even_taps, :] = w_cfm[CFM_K - 1:CFM_K, :]

    @pl.when(step + 2 < n_tiles)
    def _():
        for cp in _tile_copies(x_hbm, x_buf, in_sem, step + 2, jax.lax.rem(step + 2, X_SLOTS), True):
            cp.start()

    @pl.when(step + 1 < n_tiles)
    def _():
        for cp in _tile_copies(x_hbm, x_buf, in_sem, step + 1, x_next, True):
            cp.wait()

    @pl.when(step >= 3)
    def _():
        for cp in _tile_copies(o_hbm, o_buf, out_sem, step - 3, o_slot, False):
            cp.wait()

    @pl.when(jnp.logical_and(step > 0, step < n_tiles))
    def _():
        p_buf[0:p_halo, :] = p_buf[T:T + p_halo, :]
        ue_buf[0:s_halo, :] = ue_buf[T // 2:T // 2 + s_halo, :]
        uo_buf[0:s_halo, :] = uo_buf[T // 2:T // 2 + s_halo, :]

    @pl.when(step < n_tiles)
    def _():
        hb = hb_buf[...]
        half = nt // 2
        n_pairs = T // 2

        def correlate(buf, taps, weight, init):
            acc = init
            for m in range(taps):
                acc = acc + weight(m) * buf[nb * m:nb * m + n_pairs, :]
            return acc

        finish_previous(slice(0, half))
        z = jnp.dot(hb, w_in_ref[:, 3 * conv_w:5 * conv_w], preferred_element_type=f32)
        finish_previous(slice(half, nt))
        u = (z[:, :conv_w] * _sigmoid(z[:, conv_w:])).reshape(nt // 2, 2, nb, conv_w)
        ue_buf[s_halo:s_halo + n_pairs, :] = u[:, 0].reshape(n_pairs, conv_w)
        uo_buf[s_halo:s_halo + n_pairs, :] = u[:, 1].reshape(n_pairs, conv_w)
        d1_buf[...] = ue_buf[...] - uo_buf[...]
        d2_buf[...] = ue_buf[nb:, :] - uo_buf[0:s_halo + n_pairs - nb, :]

        shared = correlate(uo_buf, n_even_taps, lambda m: w_sum_buf[m:m + 1, :],
                           jnp.broadcast_to(b_cfm_ref[...], (n_pairs, conv_w)))
        z = jnp.dot(hb, w_in_ref[:, 1 * conv_w:3 * conv_w], preferred_element_type=f32)
        p_buf[p_halo:p_halo + T, :] = z[:, :conv_w] * z[:, conv_w:]
        short = jnp.broadcast_to(b_short_ref[...], (T, conv_w))
        for k in range(SHORT_K):
            short = short + w_short_ref[0, k:k + 1, :] * p_buf[nb * k:nb * k + T, :]
        conv_s[...] = short
        gb = jnp.dot(hb, w_in_ref[:, 0:conv_w], preferred_element_type=f32)
        sh_buf[...] = shared
        _fast_dwconv(sh_buf, d1_buf, d2_buf, w_cfm, conv_c, CFM_K, nb)

        groups = [slice(q * T // POST_SPLIT, (q + 1) * T // POST_SPLIT) for q in range(POST_SPLIT)]
        t_groups = [slice(q * nt // POST_SPLIT, (q + 1) * nt // POST_SPLIT) for q in range(POST_SPLIT)]

        def mixed_heads(r):
            uc = conv_c[r, :]
            mu = jnp.mean(uc, axis=-1, keepdims=True)
            d = uc - mu
            var = jnp.mean(d * d, axis=-1, keepdims=True)
            ln = d * jax.lax.rsqrt(var + EPS) * g_ln_ref[...] + b_ln_ref[...]
            return jnp.concatenate([gb[r] * conv_s[r, :], ln * _sigmoid(ln)], axis=-1)

        def head_scale(y):
            ms = jnp.dot((y * y).astype(bf16), seg_sum_ref[...], preferred_element_type=f32)
            r = jax.lax.rsqrt(ms + EPS)
            r_hi = r.astype(bf16)
            r_lo = (r - r_hi.astype(f32)).astype(bf16)
            return jnp.dot(jnp.concatenate([r_hi, r_lo], axis=-1), seg_bcast_ref[...], preferred_element_type=f32)

        def project(y, r_full):
            yn = y * r_full * beta_ref[...]
            return jnp.dot(yn.astype(bf16), w_out_ref[...], preferred_element_type=f32)

        def residual_and_norm(q, o):
            x1 = x_buf[x_slot, t_groups[q]].reshape(-1, D) + _per_batch_affine(_unit_rms(o), gate1)
            x1_buf[groups[q], :] = x1
            return _per_batch_affine(_unit_rms(x1), gain2, sh2).astype(bf16)

        ys, scales, outs, h2_parts = {}, {}, {}, {}
        for stage in range(POST_SPLIT + 3):
            for q in range(POST_SPLIT):
                k = stage - q
                if k == 0:
                    ys[q] = mixed_heads(groups[q])
                elif k == 1:
                    scales[q] = head_scale(ys[q])
                elif k == 2:
                    outs[q] = project(ys[q], scales[q])
                elif k == 3:
                    h2_parts[q] = residual_and_norm(q, outs[q])

        h2 = jnp.concatenate([h2_parts[q] for q in range(POST_SPLIT)], axis=0)
        n_chunks = ffn_hidden // FFN_CHUNK
        pre_rows = nt // PRE_NORM_PIECES
        for j in range(n_chunks):
            c0 = j * FFN_CHUNK
            gate = jnp.dot(h2, w_gu_ref[:, c0:c0 + FFN_CHUNK], preferred_element_type=f32)
            up = jnp.dot(h2, w_gu_ref[:, ffn_hidden + c0:ffn_hidden + c0 + FFN_CHUNK], preferred_element_type=f32)
            act = (gate * _sigmoid(gate) * up).astype(bf16)
            part = jnp.dot(act, w_down_ref[c0:c0 + FFN_CHUNK, :], preferred_element_type=f32)
            if j == 0:
                f_acc[...] = part
            else:
                f_acc[...] += part
            if j % 2 == 0 and 1 <= j // 2 <= PRE_NORM_PIECES:
                q = j // 2 - 1
                hb_buf[q * pre_rows * nb:(q + 1) * pre_rows * nb, :] = pre_norm(
                    x_next, slice(q * pre_rows, (q + 1) * pre_rows))

    @pl.when(step == n_tiles)
    def _():
        finish_previous(slice(0, nt))

    @pl.when(step >= 1)
    def _():
        for cp in _tile_copies(o_hbm, o_buf, out_sem, step - 1, o_slot, False):
            cp.start()

    @pl.when(step == n_tiles)
    def _():
        if n_tiles >= 2:
            for cp in _tile_copies(o_hbm, o_buf, out_sem, step - 2, 1 - o_slot, False):
                cp.wait()
        for cp in _tile_copies(o_hbm, o_buf, out_sem, step - 1, o_slot, False):
            cp.wait()


def _const_spec(shape):
    zeros = (0,) * len(shape)
    return pl.BlockSpec(shape, lambda s: zeros, pipeline_mode=pl.Buffered(1))


def kernel(x, c, w_ada, b_ada, g_pre_mix, g_post_mix, w_in, w_short, b_short, w_cfm_dw, b_cfm_dw,
           g_cfm_ln, b_cfm_ln, beta_mix, w_out, g_pre_ffn, g_post_ffn, w_gate_up, w_down):
    B, S, D = x.shape
    assert w_ada.shape[0] == 1, "single-layer block"
    n_mod = w_ada.shape[2] // D
    conv_w = w_short.shape[2]
    ffn_hidden = w_down.shape[1]
    assert B == V7X_SUBLANES and n_mod == 6
    assert w_in.shape[2] == 5 * conv_w and 2 * conv_w == D
    assert S % TIME_TILE == 0 and TIME_TILE % (2 * CONV_TIME) == 0 and TIME_TILE % (2 * PRE_NORM_PIECES) == 0
    assert CFM_K % 2 == 1
    assert ffn_hidden % FFN_CHUNK == 0 and ffn_hidden // FFN_CHUNK > PRE_NORM_PIECES
    assert (n_mod * D) % ADA_COLS == 0
    bf16 = jnp.bfloat16
    T = TIME_TILE * B

    mod = pl.pallas_call(
        _adaln_kernel,
        out_shape=jax.ShapeDtypeStruct((B, n_mod * D), jnp.float32),
        grid=(n_mod * D // ADA_COLS,),
        in_specs=[
            pl.BlockSpec((B, D), lambda j: (0, 0)),
            pl.BlockSpec((1, D, ADA_COLS), lambda j: (0, 0, j)),
            pl.BlockSpec((1, ADA_COLS), lambda j: (0, j)),
        ],
        out_specs=pl.BlockSpec((B, ADA_COLS), lambda j: (0, j)),
        compiler_params=pltpu.CompilerParams(dimension_semantics=("arbitrary",)),
        name="adaln",
    )(c, w_ada, b_ada)

    head = jnp.arange(D, dtype=jnp.int32) // HEAD_DIM
    member = head[:, None] == jnp.arange(V7X_LANES, dtype=jnp.int32)[None, :]
    seg_sum = jnp.where(member, 1.0 / HEAD_DIM, 0.0).astype(bf16)
    seg_bcast = jnp.tile(jnp.where(member.T, 1.0, 0.0), (2, 1)).astype(bf16)

    weights_hbm = (w_in, w_out, w_gate_up, w_down)
    for w in weights_hbm:
        assert w.shape[1] % CAST_ROWS == 0 and w.shape[2] % CAST_COLS == 0
    operands = (
        x, mod, g_pre_mix, g_post_mix, w_in, w_short, b_short,
        w_cfm_dw, b_cfm_dw, g_cfm_ln, b_cfm_ln, beta_mix, seg_sum, seg_bcast, w_out,
        g_pre_ffn, g_post_ffn, w_gate_up, w_down,
    )
    in_specs = [pl.BlockSpec(memory_space=pl.ANY) if any(t is h for h in (x,) + weights_hbm)
                else _const_spec(t.shape) for t in operands]
    n_tiles = S // TIME_TILE

    return pl.pallas_call(
        functools.partial(_layer_kernel, conv_w=conv_w, ffn_hidden=ffn_hidden, nb=B, n_tiles=n_tiles),
        out_shape=jax.ShapeDtypeStruct((B, S, D), x.dtype),
        grid=(n_tiles + 1,),
        in_specs=in_specs,
        out_specs=pl.BlockSpec(memory_space=pl.ANY),
        scratch_shapes=[
            pltpu.VMEM((X_SLOTS, TIME_TILE, B, D), jnp.float32),
            pltpu.VMEM((2, TIME_TILE, B, D), jnp.float32),
            pltpu.SemaphoreType.DMA((X_SLOTS,)),
            pltpu.SemaphoreType.DMA((2,)),
            pltpu.VMEM((B * (SHORT_K - 1) + T, conv_w), jnp.float32),
            pltpu.VMEM((B * (CFM_K - 1) // 2 + T // 2, conv_w), jnp.float32),
            pltpu.VMEM((B * (CFM_K - 1) // 2 + T // 2, conv_w), jnp.float32),
            pltpu.VMEM((B * (CFM_K - 1) // 2 + T // 2, conv_w), jnp.float32),
            pltpu.VMEM((B * (CFM_K - 1) // 2 + T // 2 - B, conv_w), jnp.float32),
            pltpu.VMEM((T // 2, conv_w), jnp.float32),
            pltpu.VMEM(((CFM_K + 1) // 2, conv_w), jnp.float32),
            pltpu.VMEM((T, conv_w), jnp.float32),
            pltpu.VMEM((T, conv_w), jnp.float32),
            pltpu.VMEM((T, D), jnp.float32),
            pltpu.VMEM((T, D), jnp.float32),
            pltpu.VMEM((T, D), jnp.bfloat16),
            pltpu.VMEM(w_in.shape[1:], jnp.bfloat16),
            pltpu.VMEM(w_out.shape[1:], jnp.bfloat16),
            pltpu.VMEM(w_gate_up.shape[1:], jnp.bfloat16),
            pltpu.VMEM(w_down.shape[1:], jnp.bfloat16),
            pltpu.VMEM((CAST_SLOTS, CAST_ROWS, CAST_COLS), jnp.float32),
            pltpu.SemaphoreType.DMA((CAST_SLOTS,)),
        ],
        compiler_params=pltpu.CompilerParams(
            dimension_semantics=("arbitrary",),
            vmem_limit_bytes=VMEM_LIMIT_BYTES,
        ),
        name="layer",
    )(*operands)
```

```python
import functools

import jax
import jax.numpy as jnp
from jax.experimental import pallas as pl
from jax.experimental.pallas import tpu as pltpu

EPS = 1e-6
HEAD_DIM = 64
SHORT_K = 3
CFM_K = 31

V7X_SUBLANES = 8
V7X_LANES = 128

TIME_TILE = 64
CONV_TIME = 8
FFN_CHUNK = 256
X_SLOTS = 3
PRE_NORM_PIECES = 8
POST_SPLIT = 2
ADA_COLS = 2048
CAST_ROWS = 256
CAST_COLS = 512
CAST_SLOTS = 10
VMEM_LIMIT_BYTES = 56 * 1024 * 1024


def _unit_rms(v):
    return v * jax.lax.rsqrt(jnp.mean(v * v, axis=-1, keepdims=True) + EPS)


def _sigmoid(v):
    return 1.0 / (1.0 + jnp.exp(-v))


def _tied_rows(v, result):
    bits = jax.lax.bitcast_convert_type(result[:, 0:V7X_LANES], jnp.uint32)
    bits = jax.lax.shift_right_logical(jax.lax.shift_right_logical(bits, jnp.uint32(16)), jnp.uint32(16))
    zero = jax.lax.bitcast_convert_type(bits, jnp.float32)
    return jnp.concatenate([v[:, 0:V7X_LANES] + zero, v[:, V7X_LANES:]], axis=1)


def _adaln_kernel(c_ref, w_ref, b_ref, o_ref):
    c = c_ref[...]
    c_act = (c * _sigmoid(c)).astype(jnp.bfloat16)
    acc = jnp.dot(c_act, w_ref[0].astype(jnp.bfloat16), preferred_element_type=jnp.float32)
    o_ref[...] = acc + b_ref[...]


def _per_batch_affine(v, scale, shift=None):
    nb = scale.shape[0]
    v3 = v.reshape(v.shape[0] // nb, nb, v.shape[1]) * scale[None]
    if shift is not None:
        v3 = v3 + shift[None]
    return v3.reshape(v.shape)


def _fast_dwconv(sh_buf, d1_buf, d2_buf, w_ref, dst_buf, n_taps, nb):
    n_e, n_o = (n_taps + 1) // 2, n_taps // 2
    rows, width = dst_buf.shape
    pairs = CONV_TIME
    blk = pairs * nb

    def corr(buf, r0, c0, taps, weight):
        win = buf[pl.ds(r0, blk + nb * (taps - 1)), c0:c0 + V7X_LANES]
        acc = weight(0) * win[0:blk]
        for m in range(1, taps):
            acc = acc + weight(m) * win[nb * m:nb * m + blk]
        return acc

    def pair_block(i, carry):
        r0 = pl.multiple_of(i * blk, blk)
        for c0 in range(0, width, V7X_LANES):
            lanes = slice(c0, c0 + V7X_LANES)
            shared = sh_buf[pl.ds(r0, blk), lanes]
            even = shared + corr(d1_buf, r0, c0, n_e, lambda m: w_ref[2 * m:2 * m + 1, lanes])
            odd = shared + corr(d2_buf, r0, c0, n_o, lambda m: w_ref[2 * m + 1:2 * m + 2, lanes])
            both = jnp.stack([even.reshape(pairs, nb, V7X_LANES), odd.reshape(pairs, nb, V7X_LANES)], axis=1)
            dst_buf[pl.ds(2 * r0, 2 * blk), lanes] = both.reshape(2 * blk, V7X_LANES)
        return carry

    jax.lax.fori_loop(0, rows // (2 * blk), pair_block, 0)


def _tile_copies(hbm_ref, buf, sem, tile, slot, to_vmem):
    nt, nb = buf.shape[1], buf.shape[2]
    copies = []
    for b in range(nb):
        hbm = hbm_ref.at[b, pl.ds(tile * nt, nt), :]
        vmem = buf.at[slot, :, b, :]
        src, dst = (hbm, vmem) if to_vmem else (vmem, hbm)
        copies.append(pltpu.make_async_copy(src, dst, sem.at[slot]))
    return copies


def _cast_weight(w_hbm, w_vmem, stage, sem):
    rows, cols = w_vmem.shape
    n_c = cols // CAST_COLS
    n = (rows // CAST_ROWS) * n_c
    ahead = CAST_SLOTS - 1

    def chunk_copy(k):
        slot = k % CAST_SLOTS
        r0, c0 = (k // n_c) * CAST_ROWS, (k % n_c) * CAST_COLS
        if not isinstance(k, int):
            r0, c0 = pl.multiple_of(r0, CAST_ROWS), pl.multiple_of(c0, CAST_COLS)
        src = w_hbm.at[0, pl.ds(r0, CAST_ROWS), pl.ds(c0, CAST_COLS)]
        return pltpu.make_async_copy(src, stage.at[slot], sem.at[slot]), slot, r0, c0

    for k in range(min(ahead, n)):
        chunk_copy(k)[0].start()

    def body(k, carry):
        @pl.when(k + ahead < n)
        def _():
            chunk_copy(k + ahead)[0].start()

        cp, slot, r0, c0 = chunk_copy(k)
        cp.wait()
        w_vmem[pl.ds(r0, CAST_ROWS), pl.ds(c0, CAST_COLS)] = stage[slot].astype(jnp.bfloat16)
        return carry

    jax.lax.fori_loop(0, n, body, 0)


def _layer_kernel(x_hbm, mod_ref, g_pre_mix_ref, g_post_mix_ref, w_in_hbm, w_short_ref, b_short_ref,
                  w_cfm_ref, b_cfm_ref, g_ln_ref, b_ln_ref, beta_ref, seg_sum_ref, seg_bcast_ref, w_out_hbm,
                  g_pre_ffn_ref, g_post_ffn_ref, w_gu_hbm, w_down_hbm,
                  o_hbm,
                  x_buf, o_buf, in_sem, out_sem, p_buf, ue_buf, uo_buf, d1_buf, d2_buf, sh_buf, w_sum_buf, conv_s, conv_c,
                  f_acc, x1_buf, hb_buf, w_in_ref, w_out_ref, w_gu_ref, w_down_ref, cast_stage, cast_sem,
                  *, conv_w, ffn_hidden, nb, n_tiles):
    step = pl.program_id(0)
    x_slot = jax.lax.rem(step, X_SLOTS)
    x_next = jax.lax.rem(step + 1, X_SLOTS)
    o_slot = jax.lax.rem(step + 1, 2)
    nt, D = x_buf.shape[1], x_buf.shape[3]
    T = nt * nb
    bf16 = jnp.bfloat16
    f32 = jnp.float32
    p_halo = nb * (SHORT_K - 1)
    s_halo = nb * (CFM_K - 1) // 2
    n_even_taps = (CFM_K + 1) // 2
    w_cfm = w_cfm_ref.at[0]

    mod = mod_ref[...]
    sh1, sc1, gt1, sh2, sc2, gt2 = (mod[:, i * D:(i + 1) * D] for i in range(6))
    gain1 = g_pre_mix_ref[...] * (1.0 + sc1)
    gate1 = g_post_mix_ref[...] * gt1
    gain2 = g_pre_ffn_ref[...] * (1.0 + sc2)
    gate2 = g_post_ffn_ref[...] * gt2

    def pre_norm(slot, rows, beside=None):
        xt = x_buf[slot, rows].reshape(-1, D)
        if beside is not None:
            xt = _tied_rows(xt, beside)
        return _per_batch_affine(_unit_rms(xt), gain1, sh1).astype(bf16)

    def finish_previous(rows):
        r = slice(rows.start * nb, rows.stop * nb)
        out = x1_buf[r, :] + _per_batch_affine(_unit_rms(f_acc[r, :]), gate2)
        o_buf[o_slot, rows] = out.reshape(rows.stop - rows.start, nb, D)

    @pl.when(step == 0)
    def _():
        for t in range(min(2, n_tiles)):
            for cp in _tile_copies(x_hbm, x_buf, in_sem, t, t, True):
                cp.start()
        for w_hbm, w_ref in ((w_in_hbm, w_in_ref), (w_out_hbm, w_out_ref), (w_gu_hbm, w_gu_ref),
                             (w_down_hbm, w_down_ref)):
            _cast_weight(w_hbm, w_ref, cast_stage, cast_sem)
        for cp in _tile_copies(x_hbm, x_buf, in_sem, 0, 0, True):
            cp.wait()
        hb_buf[...] = pre_norm(0, slice(0, nt))
        f_acc[...] = jnp.zeros(f_acc.shape, f32)
        x1_buf[...] = jnp.zeros(x1_buf.shape, f32)
        p_buf[0:p_halo, :] = jnp.zeros((p_halo, conv_w), f32)
        ue_buf[0:s_halo, :] = jnp.zeros((s_halo, conv_w), f32)
        uo_buf[0:s_halo, :] = jnp.zeros((s_halo, conv_w), f32)
        for m in range(n_even_taps - 1):
            w_sum_buf[m:m + 1, :] = w_cfm[2 * m:2 * m + 1, :] + w_cfm[2 * m + 1:2 * m + 2, :]
        w_sum_buf[n_even_taps - 1:n_even_taps, :] = w_cfm[CFM_K - 1:CFM_K, :]

    @pl.when(step + 2 < n_tiles)
    def _():
        for cp in _tile_copies(x_hbm, x_buf, in_sem, step + 2, jax.lax.rem(step + 2, X_SLOTS), True):
            cp.start()

    @pl.when(step + 1 < n_tiles)
    def _():
        for cp in _tile_copies(x_hbm, x_buf, in_sem, step + 1, x_next, True):
            cp.wait()

    @pl.when(step >= 3)
    def _():
        for cp in _tile_copies(o_hbm, o_buf, out_sem, step - 3, o_slot, False):
            cp.wait()

    @pl.when(jnp.logical_and(step > 0, step < n_tiles))
    def _():
        p_buf[0:p_halo, :] = p_buf[T:T + p_halo, :]
        ue_buf[0:s_halo, :] = ue_buf[T // 2:T // 2 + s_halo, :]
        uo_buf[0:s_halo, :] = uo_buf[T // 2:T // 2 + s_halo, :]

    @pl.when(step < n_tiles)
    def _():
        hb = hb_buf[...]
        half = nt // 2
        n_pairs = T // 2

        def correlate(buf, taps, weight, init):
            acc = init
            for m in range(taps):
                acc = acc + weight(m) * buf[nb * m:nb * m + n_pairs, :]
            return acc

        finish_previous(slice(0, half))
        z = jnp.dot(hb, w_in_ref[:, 3 * conv_w:5 * conv_w], preferred_element_type=f32)
        finish_previous(slice(half, nt))
        u = (z[:, :conv_w] * _sigmoid(z[:, conv_w:])).reshape(nt // 2, 2, nb, conv_w)
        ue_buf[s_halo:s_halo + n_pairs, :] = u[:, 0].reshape(n_pairs, conv_w)
        uo_buf[s_halo:s_halo + n_pairs, :] = u[:, 1].reshape(n_pairs, conv_w)
        d1_buf[...] = ue_buf[...] - uo_buf[...]
        d2_buf[...] = ue_buf[nb:, :] - uo_buf[0:s_halo + n_pairs - nb, :]

        shared = correlate(uo_buf, n_even_taps, lambda m: w_sum_buf[m:m + 1, :],
                           jnp.broadcast_to(b_cfm_ref[...], (n_pairs, conv_w)))
        z = jnp.dot(hb, w_in_ref[:, 1 * conv_w:3 * conv_w], preferred_element_type=f32)
        p_buf[p_halo:p_halo + T, :] = z[:, :conv_w] * z[:, conv_w:]
        short = jnp.broadcast_to(b_short_ref[...], (T, conv_w))
        for k in range(SHORT_K):
            short = short + w_short_ref[0, k:k + 1, :] * p_buf[nb * k:nb * k + T, :]
        conv_s[...] = short
        gb = jnp.dot(hb, w_in_ref[:, 0:conv_w], preferred_element_type=f32)
        sh_buf[...] = shared
        _fast_dwconv(sh_buf, d1_buf, d2_buf, w_cfm, conv_c, CFM_K, nb)

        groups = [slice(q * T // POST_SPLIT, (q + 1) * T // POST_SPLIT) for q in range(POST_SPLIT)]
        t_groups = [slice(q * nt // POST_SPLIT, (q + 1) * nt // POST_SPLIT) for q in range(POST_SPLIT)]

        def mixed_heads(r):
            uc = conv_c[r, :]
            mu = jnp.mean(uc, axis=-1, keepdims=True)
            d = uc - mu
            var = jnp.mean(d * d, axis=-1, keepdims=True)
            ln = d * jax.lax.rsqrt(var + EPS) * g_ln_ref[...] + b_ln_ref[...]
            return jnp.concatenate([gb[r] * conv_s[r, :], ln * _sigmoid(ln)], axis=-1)

        def head_scale(y):
            ms = jnp.dot((y * y).astype(bf16), seg_sum_ref[...], preferred_element_type=f32)
            r = jax.lax.rsqrt(ms + EPS)
            r_hi = r.astype(bf16)
            r_lo = (r - r_hi.astype(f32)).astype(bf16)
            return jnp.dot(jnp.concatenate([r_hi, r_lo], axis=-1), seg_bcast_ref[...], preferred_element_type=f32)

        def project(y, r_full):
            yn = y * r_full * beta_ref[...]
            return jnp.dot(yn.astype(bf16), w_out_ref[...], preferred_element_type=f32)

        def residual_and_norm(q, o):
            x1 = x_buf[x_slot, t_groups[q]].reshape(-1, D) + _per_batch_affine(_unit_rms(o), gate1)
            x1_buf[groups[q], :] = x1
            return _per_batch_affine(_unit_rms(x1), gain2, sh2).astype(bf16)

        ys, scales, outs, h2_parts = {}, {}, {}, {}
        for stage in range(POST_SPLIT + 3):
            for q in range(POST_SPLIT):
                k = stage - q
                if k == 0:
                    ys[q] = mixed_heads(groups[q])
                elif k == 1:
                    scales[q] = head_scale(ys[q])
                elif k == 2:
                    outs[q] = project(ys[q], scales[q])
                elif k == 3:
                    h2_parts[q] = residual_and_norm(q, outs[q])

        h2 = jnp.concatenate([h2_parts[q] for q in range(POST_SPLIT)], axis=0)
        n_chunks = ffn_hidden // FFN_CHUNK
        pre_rows = nt // PRE_NORM_PIECES
        for j in range(n_chunks):
            c0 = j * FFN_CHUNK
            gate = jnp.dot(h2, w_gu_ref[:, c0:c0 + FFN_CHUNK], preferred_element_type=f32)
            up = jnp.dot(h2, w_gu_ref[:, ffn_hidden + c0:ffn_hidden + c0 + FFN_CHUNK], preferred_element_type=f32)
            act = (gate * _sigmoid(gate) * up).astype(bf16)
            part = jnp.dot(act, w_down_ref[c0:c0 + FFN_CHUNK, :], preferred_element_type=f32)
            if j == 0:
                f_acc[...] = part
            else:
                f_acc[...] += part
            if 1 <= j <= PRE_NORM_PIECES:
                q = j - 1
                r = slice(q * pre_rows * nb, (q + 1) * pre_rows * nb)
                hb_buf[r, :] = pre_norm(x_next, slice(q * pre_rows, (q + 1) * pre_rows), part[r, :])

    @pl.when(step == n_tiles)
    def _():
        finish_previous(slice(0, nt))

    @pl.when(step >= 1)
    def _():
        for cp in _tile_copies(o_hbm, o_buf, out_sem, step - 1, o_slot, False):
            cp.start()

    @pl.when(step == n_tiles)
    def _():
        if n_tiles >= 2:
            for cp in _tile_copies(o_hbm, o_buf, out_sem, step - 2, 1 - o_slot, False):
                cp.wait()
        for cp in _tile_copies(o_hbm, o_buf, out_sem, step - 1, o_slot, False):
            cp.wait()


def _const_spec(shape):
    zeros = (0,) * len(shape)
    return pl.BlockSpec(shape, lambda s: zeros, pipeline_mode=pl.Buffered(1))


def kernel(x, c, w_ada, b_ada, g_pre_mix, g_post_mix, w_in, w_short, b_short, w_cfm_dw, b_cfm_dw,
           g_cfm_ln, b_cfm_ln, beta_mix, w_out, g_pre_ffn, g_post_ffn, w_gate_up, w_down):
    B, S, D = x.shape
    assert w_ada.shape[0] == 1, "single-layer block"
    n_mod = w_ada.shape[2] // D
    conv_w = w_short.shape[2]
    ffn_hidden = w_down.shape[1]
    assert B == V7X_SUBLANES and n_mod == 6
    assert w_in.shape[2] == 5 * conv_w and 2 * conv_w == D
    assert S % TIME_TILE == 0 and TIME_TILE % (2 * CONV_TIME) == 0 and TIME_TILE % (2 * PRE_NORM_PIECES) == 0
    assert CFM_K % 2 == 1
    assert ffn_hidden % FFN_CHUNK == 0 and ffn_hidden // FFN_CHUNK > PRE_NORM_PIECES
    assert (n_mod * D) % ADA_COLS == 0
    bf16 = jnp.bfloat16
    T = TIME_TILE * B

    mod = pl.pallas_call(
        _adaln_kernel,
        out_shape=jax.ShapeDtypeStruct((B, n_mod * D), jnp.float32),
        grid=(n_mod * D // ADA_COLS,),
        in_specs=[
            pl.BlockSpec((B, D), lambda j: (0, 0)),
            pl.BlockSpec((1, D, ADA_COLS), lambda j: (0, 0, j)),
            pl.BlockSpec((1, ADA_COLS), lambda j: (0, j)),
        ],
        out_specs=pl.BlockSpec((B, ADA_COLS), lambda j: (0, j)),
        compiler_params=pltpu.CompilerParams(dimension_semantics=("arbitrary",)),
        name="adaln",
    )(c, w_ada, b_ada)

    head = jnp.arange(D, dtype=jnp.int32) // HEAD_DIM
    member = head[:, None] == jnp.arange(V7X_LANES, dtype=jnp.int32)[None, :]
    seg_sum = jnp.where(member, 1.0 / HEAD_DIM, 0.0).astype(bf16)
    seg_bcast = jnp.tile(jnp.where(member.T, 1.0, 0.0), (2, 1)).astype(bf16)

    weights_hbm = (w_in, w_out, w_gate_up, w_down)
    for w in weights_hbm:
        assert w.shape[1] % CAST_ROWS == 0 and w.shape[2] % CAST_COLS == 0
    operands = (
        x, mod, g_pre_mix, g_post_mix, w_in, w_short, b_short,
        w_cfm_dw, b_cfm_dw, g_cfm_ln, b_cfm_ln, beta_mix, seg_sum, seg_bcast, w_out,
        g_pre_ffn, g_post_ffn, w_gate_up, w_down,
    )
    in_specs = [pl.BlockSpec(memory_space=pl.ANY) if any(t is h for h in (x,) + weights_hbm)
                else _const_spec(t.shape) for t in operands]
    n_tiles = S // TIME_TILE

    return pl.pallas_call(
        functools.partial(_layer_kernel, conv_w=conv_w, ffn_hidden=ffn_hidden, nb=B, n_tiles=n_tiles),
        out_shape=jax.ShapeDtypeStruct((B, S, D), x.dtype),
        grid=(n_tiles + 1,),
        in_specs=in_specs,
        out_specs=pl.BlockSpec(memory_space=pl.ANY),
        scratch_shapes=[
            pltpu.VMEM((X_SLOTS, TIME_TILE, B, D), jnp.float32),
            pltpu.VMEM((2, TIME_TILE, B, D), jnp.float32),
            pltpu.SemaphoreType.DMA((X_SLOTS,)),
            pltpu.SemaphoreType.DMA((2,)),
            pltpu.VMEM((B * (SHORT_K - 1) + T, conv_w), jnp.float32),
            pltpu.VMEM((B * (CFM_K - 1) // 2 + T // 2, conv_w), jnp.float32),
            pltpu.VMEM((B * (CFM_K - 1) // 2 + T // 2, conv_w), jnp.float32),
            pltpu.VMEM((B * (CFM_K - 1) // 2 + T // 2, conv_w), jnp.float32),
            pltpu.VMEM((B * (CFM_K - 1) // 2 + T // 2 - B, conv_w), jnp.float32),
            pltpu.VMEM((T // 2, conv_w), jnp.float32),
            pltpu.VMEM(((CFM_K + 1) // 2, conv_w), jnp.float32),
            pltpu.VMEM((T, conv_w), jnp.float32),
            pltpu.VMEM((T, conv_w), jnp.float32),
            pltpu.VMEM((T, D), jnp.float32),
            pltpu.VMEM((T, D), jnp.float32),
            pltpu.VMEM((T, D), jnp.bfloat16),
            pltpu.VMEM(w_in.shape[1:], jnp.bfloat16),
            pltpu.VMEM(w_out.shape[1:], jnp.bfloat16),
            pltpu.VMEM(w_gate_up.shape[1:], jnp.bfloat16),
            pltpu.VMEM(w_down.shape[1:], jnp.bfloat16),
            pltpu.VMEM((CAST_SLOTS, CAST_ROWS, CAST_COLS), jnp.float32),
            pltpu.SemaphoreType.DMA((CAST_SLOTS,)),
        ],
        compiler_params=pltpu.CompilerParams(
            dimension_semantics=("arbitrary",),
            vmem_limit_bytes=VMEM_LIMIT_BYTES,
        ),
        name="layer",
    )(*operands)
```

```python
import functools

import jax
import jax.numpy as jnp
from jax.experimental import pallas as pl
from jax.experimental.pallas import tpu as pltpu

EPS = 1e-6
HEAD_DIM = 64
SHORT_K = 3
CFM_K = 31

V7X_SUBLANES = 8
V7X_LANES = 128

TIME_TILE = 64
CONV_TIME = 8
FFN_CHUNK = 256
X_SLOTS = 3
O_SLOTS = 3
FINISH_PIECES = 4
PRE_NORM_PIECES = 4
POST_SPLIT = 2
ADA_COLS = 2048
CAST_ROWS = 256
CAST_COLS = 512
CAST_SLOTS = 10
VMEM_LIMIT_BYTES = 58 * 1024 * 1024


def _unit_rms(v):
    return v * jax.lax.rsqrt(jnp.mean(v * v, axis=-1, keepdims=True) + EPS)


def _sigmoid(v):
    return 1.0 / (1.0 + jnp.exp(-v))


def _tied_rows(v, result):
    bits = jax.lax.bitcast_convert_type(result[:, 0:V7X_LANES], jnp.uint32)
    bits = jax.lax.shift_right_logical(jax.lax.shift_right_logical(bits, jnp.uint32(16)), jnp.uint32(16))
    zero = jax.lax.bitcast_convert_type(bits, jnp.float32)
    return jnp.concatenate([v[:, 0:V7X_LANES] + zero, v[:, V7X_LANES:]], axis=1)


def _adaln_kernel(c_ref, w_ref, b_ref, o_ref):
    c = c_ref[...]
    c_act = (c * _sigmoid(c)).astype(jnp.bfloat16)
    acc = jnp.dot(c_act, w_ref[0].astype(jnp.bfloat16), preferred_element_type=jnp.float32)
    o_ref[...] = acc + b_ref[...]


def _per_batch_affine(v, scale, shift=None):
    nb = scale.shape[0]
    v3 = v.reshape(v.shape[0] // nb, nb, v.shape[1]) * scale[None]
    if shift is not None:
        v3 = v3 + shift[None]
    return v3.reshape(v.shape)


def _fast_dwconv(sh_buf, d1_buf, d2_buf, w_ref, dst_buf, n_taps, nb):
    n_e, n_o = (n_taps + 1) // 2, n_taps // 2
    rows, width = dst_buf.shape
    pairs = CONV_TIME
    blk = pairs * nb

    def corr(buf, r0, c0, taps, weight):
        win = buf[pl.ds(r0, blk + nb * (taps - 1)), c0:c0 + V7X_LANES]
        acc = weight(0) * win[0:blk]
        for m in range(1, taps):
            acc = acc + weight(m) * win[nb * m:nb * m + blk]
        return acc

    def pair_block(i, carry):
        r0 = pl.multiple_of(i * blk, blk)
        for c0 in range(0, width, V7X_LANES):
            lanes = slice(c0, c0 + V7X_LANES)
            shared = sh_buf[pl.ds(r0, blk), lanes]
            even = shared + corr(d1_buf, r0, c0, n_e, lambda m: w_ref[2 * m:2 * m + 1, lanes])
            odd = shared + corr(d2_buf, r0, c0, n_o, lambda m: w_ref[2 * m + 1:2 * m + 2, lanes])
            both = jnp.stack([even.reshape(pairs, nb, V7X_LANES), odd.reshape(pairs, nb, V7X_LANES)], axis=1)
            dst_buf[pl.ds(2 * r0, 2 * blk), lanes] = both.reshape(2 * blk, V7X_LANES)
        return carry

    jax.lax.fori_loop(0, rows // (2 * blk), pair_block, 0)


def _tile_copies(hbm_ref, buf, sem, tile, slot, to_vmem):
    nt, nb = buf.shape[1], buf.shape[2]
    copies = []
    for b in range(nb):
        hbm = hbm_ref.at[b, pl.ds(tile * nt, nt), :]
        vmem = buf.at[slot, :, b, :]
        src, dst = (hbm, vmem) if to_vmem else (vmem, hbm)
        copies.append(pltpu.make_async_copy(src, dst, sem.at[slot]))
    return copies


def _cast_weight(w_hbm, w_vmem, stage, sem):
    rows, cols = w_vmem.shape
    n_c = cols // CAST_COLS
    n = (rows // CAST_ROWS) * n_c
    ahead = CAST_SLOTS - 1

    def chunk_copy(k):
        slot = k % CAST_SLOTS
        r0, c0 = (k // n_c) * CAST_ROWS, (k % n_c) * CAST_COLS
        if not isinstance(k, int):
            r0, c0 = pl.multiple_of(r0, CAST_ROWS), pl.multiple_of(c0, CAST_COLS)
        src = w_hbm.at[0, pl.ds(r0, CAST_ROWS), pl.ds(c0, CAST_COLS)]
        return pltpu.make_async_copy(src, stage.at[slot], sem.at[slot]), slot, r0, c0

    for k in range(min(ahead, n)):
        chunk_copy(k)[0].start()

    def body(k, carry):
        @pl.when(k + ahead < n)
        def _():
            chunk_copy(k + ahead)[0].start()

        cp, slot, r0, c0 = chunk_copy(k)
        cp.wait()
        w_vmem[pl.ds(r0, CAST_ROWS), pl.ds(c0, CAST_COLS)] = stage[slot].astype(jnp.bfloat16)
        return carry

    jax.lax.fori_loop(0, n, body, 0)


def _layer_kernel(x_hbm, mod_ref, g_pre_mix_ref, g_post_mix_ref, w_in_hbm, w_short_ref, b_short_ref,
                  w_cfm_ref, b_cfm_ref, g_ln_ref, b_ln_ref, beta_ref, seg_sum_ref, seg_bcast_ref, w_out_hbm,
                  g_pre_ffn_ref, g_post_ffn_ref, w_gu_hbm, w_down_hbm,
                  o_hbm,
                  x_buf, o_buf, in_sem, out_sem, p_buf, ue_buf, uo_buf, d1_buf, d2_buf, sh_buf, w_sum_buf, conv_s, conv_c,
                  f_acc, f_done, hb_buf, w_in_ref, w_out_ref, w_gu_ref, w_down_ref, cast_stage, cast_sem, f_sem,
                  *, conv_w, ffn_hidden, nb, n_tiles):
    step = pl.program_id(0)
    x_slot = jax.lax.rem(step, X_SLOTS)
    x_next = jax.lax.rem(step + 1, X_SLOTS)
    o_cur = jax.lax.rem(step, O_SLOTS)
    o_slot = jax.lax.rem(step + O_SLOTS - 1, O_SLOTS)
    o_older = jax.lax.rem(step + O_SLOTS - 2, O_SLOTS)
    f_snapshot = pltpu.make_async_copy(f_acc, f_done, f_sem.at[0])
    nt, D = x_buf.shape[1], x_buf.shape[3]
    T = nt * nb
    bf16 = jnp.bfloat16
    f32 = jnp.float32
    p_halo = nb * (SHORT_K - 1)
    s_halo = nb * (CFM_K - 1) // 2
    n_even_taps = (CFM_K + 1) // 2
    w_cfm = w_cfm_ref.at[0]

    mod = mod_ref[...]
    sh1, sc1, gt1, sh2, sc2, gt2 = (mod[:, i * D:(i + 1) * D] for i in range(6))
    gain1 = g_pre_mix_ref[...] * (1.0 + sc1)
    gate1 = g_post_mix_ref[...] * gt1
    gain2 = g_pre_ffn_ref[...] * (1.0 + sc2)
    gate2 = g_post_ffn_ref[...] * gt2

    def pre_norm(slot, rows, beside=None):
        xt = x_buf[slot, rows].reshape(-1, D)
        if beside is not None:
            xt = _tied_rows(xt, beside)
        return _per_batch_affine(_unit_rms(xt), gain1, sh1).astype(bf16)

    def finish_previous(rows, beside=None):
        r = slice(rows.start * nb, rows.stop * nb)
        f = f_done[r, :] if beside is None else _tied_rows(f_done[r, :], beside)
        out = o_buf[o_slot, rows].reshape(-1, D) + _per_batch_affine(_unit_rms(f), gate2)
        o_buf[o_slot, rows] = out.reshape(rows.stop - rows.start, nb, D)

    @pl.when(step == 0)
    def _():
        for t in range(min(2, n_tiles)):
            for cp in _tile_copies(x_hbm, x_buf, in_sem, t, t, True):
                cp.start()
        for w_hbm, w_ref in ((w_in_hbm, w_in_ref), (w_out_hbm, w_out_ref), (w_gu_hbm, w_gu_ref),
                             (w_down_hbm, w_down_ref)):
            _cast_weight(w_hbm, w_ref, cast_stage, cast_sem)
        for cp in _tile_copies(x_hbm, x_buf, in_sem, 0, 0, True):
            cp.wait()
        hb_buf[...] = pre_norm(0, slice(0, nt))
        f_done[...] = jnp.zeros(f_done.shape, f32)
        o_buf[O_SLOTS - 1] = jnp.zeros(o_buf.shape[1:], f32)
        p_buf[0:p_halo, :] = jnp.zeros((p_halo, conv_w), f32)
        ue_buf[0:s_halo, :] = jnp.zeros((s_halo, conv_w), f32)
        uo_buf[0:s_halo, :] = jnp.zeros((s_halo, conv_w), f32)
        for m in range(n_even_taps - 1):
            w_sum_buf[m:m + 1, :] = w_cfm[2 * m:2 * m + 1, :] + w_cfm[2 * m + 1:2 * m + 2, :]
        w_sum_buf[n_even_taps - 1:n_even_taps, :] = w_cfm[CFM_K - 1:CFM_K, :]

    @pl.when(step + 2 < n_tiles)
    def _():
        for cp in _tile_copies(x_hbm, x_buf, in_sem, step + 2, jax.lax.rem(step + 2, X_SLOTS), True):
            cp.start()

    @pl.when(step + 1 < n_tiles)
    def _():
        for cp in _tile_copies(x_hbm, x_buf, in_sem, step + 1, x_next, True):
            cp.wait()

    @pl.when(step >= O_SLOTS)
    def _():
        for cp in _tile_copies(o_hbm, o_buf, out_sem, step - O_SLOTS, o_cur, False):
            cp.wait()

    @pl.when(jnp.logical_and(step > 0, step < n_tiles))
    def _():
        p_buf[0:p_halo, :] = p_buf[T:T + p_halo, :]
        ue_buf[0:s_halo, :] = ue_buf[T // 2:T // 2 + s_halo, :]
        uo_buf[0:s_halo, :] = uo_buf[T // 2:T // 2 + s_halo, :]

    @pl.when(step < n_tiles)
    def _():
        hb = hb_buf[...]
        n_pairs = T // 2

        def correlate(buf, taps, weight, init):
            acc = init
            for m in range(taps):
                acc = acc + weight(m) * buf[nb * m:nb * m + n_pairs, :]
            return acc

        z = jnp.dot(hb, w_in_ref[:, 3 * conv_w:5 * conv_w], preferred_element_type=f32)
        u = (z[:, :conv_w] * _sigmoid(z[:, conv_w:])).reshape(nt // 2, 2, nb, conv_w)
        ue_buf[s_halo:s_halo + n_pairs, :] = u[:, 0].reshape(n_pairs, conv_w)
        uo_buf[s_halo:s_halo + n_pairs, :] = u[:, 1].reshape(n_pairs, conv_w)
        d1_buf[...] = ue_buf[...] - uo_buf[...]
        d2_buf[...] = ue_buf[nb:, :] - uo_buf[0:s_halo + n_pairs - nb, :]

        shared = correlate(uo_buf, n_even_taps, lambda m: w_sum_buf[m:m + 1, :],
                           jnp.broadcast_to(b_cfm_ref[...], (n_pairs, conv_w)))
        z = jnp.dot(hb, w_in_ref[:, 1 * conv_w:3 * conv_w], preferred_element_type=f32)
        p_buf[p_halo:p_halo + T, :] = z[:, :conv_w] * z[:, conv_w:]
        short = jnp.broadcast_to(b_short_ref[...], (T, conv_w))
        for k in range(SHORT_K):
            short = short + w_short_ref[0, k:k + 1, :] * p_buf[nb * k:nb * k + T, :]
        conv_s[...] = short
        gb = jnp.dot(hb, w_in_ref[:, 0:conv_w], preferred_element_type=f32)
        sh_buf[...] = shared
        _fast_dwconv(sh_buf, d1_buf, d2_buf, w_cfm, conv_c, CFM_K, nb)

        groups = [slice(q * T // POST_SPLIT, (q + 1) * T // POST_SPLIT) for q in range(POST_SPLIT)]
        t_groups = [slice(q * nt // POST_SPLIT, (q + 1) * nt // POST_SPLIT) for q in range(POST_SPLIT)]

        def mixed_heads(r):
            uc = conv_c[r, :]
            mu = jnp.mean(uc, axis=-1, keepdims=True)
            d = uc - mu
            var = jnp.mean(d * d, axis=-1, keepdims=True)
            ln = d * jax.lax.rsqrt(var + EPS) * g_ln_ref[...] + b_ln_ref[...]
            return jnp.concatenate([gb[r] * conv_s[r, :], ln * _sigmoid(ln)], axis=-1)

        def head_scale(y):
            ms = jnp.dot((y * y).astype(bf16), seg_sum_ref[...], preferred_element_type=f32)
            r = jax.lax.rsqrt(ms + EPS)
            r_hi = r.astype(bf16)
            r_lo = (r - r_hi.astype(f32)).astype(bf16)
            return jnp.dot(jnp.concatenate([r_hi, r_lo], axis=-1), seg_bcast_ref[...], preferred_element_type=f32)

        def project(y, r_full):
            yn = y * r_full * beta_ref[...]
            return jnp.dot(yn.astype(bf16), w_out_ref[...], preferred_element_type=f32)

        def residual_and_norm(q, o):
            x1 = x_buf[x_slot, t_groups[q]].reshape(-1, D) + _per_batch_affine(_unit_rms(o), gate1)
            o_buf[o_cur, t_groups[q]] = x1.reshape(nt // POST_SPLIT, nb, D)
            return _per_batch_affine(_unit_rms(x1), gain2, sh2).astype(bf16)

        ys, scales, outs, h2_parts = {}, {}, {}, {}
        for stage in range(POST_SPLIT + 3):
            for q in range(POST_SPLIT):
                k = stage - q
                if k == 0:
                    ys[q] = mixed_heads(groups[q])
                elif k == 1:
                    scales[q] = head_scale(ys[q])
                elif k == 2:
                    outs[q] = project(ys[q], scales[q])
                elif k == 3:
                    h2_parts[q] = residual_and_norm(q, outs[q])

        h2 = jnp.concatenate([h2_parts[q] for q in range(POST_SPLIT)], axis=0)
        n_chunks = ffn_hidden // FFN_CHUNK
        pre_rows = nt // PRE_NORM_PIECES
        fin_rows = nt // FINISH_PIECES

        @pl.when(step > 0)
        def _():
            f_snapshot.wait()
        for j in range(n_chunks):
            c0 = j * FFN_CHUNK
            gate = jnp.dot(h2, w_gu_ref[:, c0:c0 + FFN_CHUNK], preferred_element_type=f32)
            up = jnp.dot(h2, w_gu_ref[:, ffn_hidden + c0:ffn_hidden + c0 + FFN_CHUNK], preferred_element_type=f32)
            act = (gate * _sigmoid(gate) * up).astype(bf16)
            part = jnp.dot(act, w_down_ref[c0:c0 + FFN_CHUNK, :], preferred_element_type=f32)
            if j == 0:
                f_acc[...] = part
            else:
                f_acc[...] += part
            q = j // 2
            if j % 2 == 0 and q < PRE_NORM_PIECES:
                r = slice(q * pre_rows * nb, (q + 1) * pre_rows * nb)
                hb_buf[r, :] = pre_norm(x_next, slice(q * pre_rows, (q + 1) * pre_rows), part[r, :])
            elif j % 2 == 1 and q < FINISH_PIECES:
                r = slice(q * fin_rows * nb, (q + 1) * fin_rows * nb)
                finish_previous(slice(q * fin_rows, (q + 1) * fin_rows), part[r, :])
        f_snapshot.start()

    @pl.when(step == n_tiles)
    def _():
        f_snapshot.wait()
        finish_previous(slice(0, nt))

    @pl.when(step >= 1)
    def _():
        for cp in _tile_copies(o_hbm, o_buf, out_sem, step - 1, o_slot, False):
            cp.start()

    @pl.when(step == n_tiles)
    def _():
        if n_tiles >= 2:
            for cp in _tile_copies(o_hbm, o_buf, out_sem, step - 2, o_older, False):
                cp.wait()
        for cp in _tile_copies(o_hbm, o_buf, out_sem, step - 1, o_slot, False):
            cp.wait()


def _const_spec(shape):
    zeros = (0,) * len(shape)
    return pl.BlockSpec(shape, lambda s: zeros, pipeline_mode=pl.Buffered(1))


def kernel(x, c, w_ada, b_ada, g_pre_mix, g_post_mix, w_in, w_short, b_short, w_cfm_dw, b_cfm_dw,
           g_cfm_ln, b_cfm_ln, beta_mix, w_out, g_pre_ffn, g_post_ffn, w_gate_up, w_down):
    B, S, D = x.shape
    assert w_ada.shape[0] == 1, "single-layer block"
    n_mod = w_ada.shape[2] // D
    conv_w = w_short.shape[2]
    ffn_hidden = w_down.shape[1]
    assert B == V7X_SUBLANES and n_mod == 6
    assert w_in.shape[2] == 5 * conv_w and 2 * conv_w == D
    assert S % TIME_TILE == 0 and TIME_TILE % (2 * CONV_TIME) == 0 and TIME_TILE % (2 * PRE_NORM_PIECES) == 0
    assert CFM_K % 2 == 1
    assert TIME_TILE % FINISH_PIECES == 0
    assert ffn_hidden % FFN_CHUNK == 0 and ffn_hidden // FFN_CHUNK >= 2 * max(PRE_NORM_PIECES, FINISH_PIECES)
    assert (n_mod * D) % ADA_COLS == 0
    bf16 = jnp.bfloat16
    T = TIME_TILE * B

    mod = pl.pallas_call(
        _adaln_kernel,
        out_shape=jax.ShapeDtypeStruct((B, n_mod * D), jnp.float32),
        grid=(n_mod * D // ADA_COLS,),
        in_specs=[
            pl.BlockSpec((B, D), lambda j: (0, 0)),
            pl.BlockSpec((1, D, ADA_COLS), lambda j: (0, 0, j)),
            pl.BlockSpec((1, ADA_COLS), lambda j: (0, j)),
        ],
        out_specs=pl.BlockSpec((B, ADA_COLS), lambda j: (0, j)),
        compiler_params=pltpu.CompilerParams(dimension_semantics=("arbitrary",)),
        name="adaln",
    )(c, w_ada, b_ada)

    head = jnp.arange(D, dtype=jnp.int32) // HEAD_DIM
    member = head[:, None] == jnp.arange(V7X_LANES, dtype=jnp.int32)[None, :]
    seg_sum = jnp.where(member, 1.0 / HEAD_DIM, 0.0).astype(bf16)
    seg_bcast = jnp.tile(jnp.where(member.T, 1.0, 0.0), (2, 1)).astype(bf16)

    weights_hbm = (w_in, w_out, w_gate_up, w_down)
    for w in weights_hbm:
        assert w.shape[1] % CAST_ROWS == 0 and w.shape[2] % CAST_COLS == 0
    operands = (
        x, mod, g_pre_mix, g_post_mix, w_in, w_short, b_short,
        w_cfm_dw, b_cfm_dw, g_cfm_ln, b_cfm_ln, beta_mix, seg_sum, seg_bcast, w_out,
        g_pre_ffn, g_post_ffn, w_gate_up, w_down,
    )
    in_specs = [pl.BlockSpec(memory_space=pl.ANY) if any(t is h for h in (x,) + weights_hbm)
                else _const_spec(t.shape) for t in operands]
    n_tiles = S // TIME_TILE

    return pl.pallas_call(
        functools.partial(_layer_kernel, conv_w=conv_w, ffn_hidden=ffn_hidden, nb=B, n_tiles=n_tiles),
        out_shape=jax.ShapeDtypeStruct((B, S, D), x.dtype),
        grid=(n_tiles + 1,),
        in_specs=in_specs,
        out_specs=pl.BlockSpec(memory_space=pl.ANY),
        scratch_shapes=[
            pltpu.VMEM((X_SLOTS, TIME_TILE, B, D), jnp.float32),
            pltpu.VMEM((O_SLOTS, TIME_TILE, B, D), jnp.float32),
            pltpu.SemaphoreType.DMA((X_SLOTS,)),
            pltpu.SemaphoreType.DMA((O_SLOTS,)),
            pltpu.VMEM((B * (SHORT_K - 1) + T, conv_w), jnp.float32),
            pltpu.VMEM((B * (CFM_K - 1) // 2 + T // 2, conv_w), jnp.float32),
            pltpu.VMEM((B * (CFM_K - 1) // 2 + T // 2, conv_w), jnp.float32),
            pltpu.VMEM((B * (CFM_K - 1) // 2 + T // 2, conv_w), jnp.float32),
            pltpu.VMEM((B * (CFM_K - 1) // 2 + T // 2 - B, conv_w), jnp.float32),
            pltpu.VMEM((T // 2, conv_w), jnp.float32),
            pltpu.VMEM(((CFM_K + 1) // 2, conv_w), jnp.float32),
            pltpu.VMEM((T, conv_w), jnp.float32),
            pltpu.VMEM((T, conv_w), jnp.float32),
            pltpu.VMEM((T, D), jnp.float32),
            pltpu.VMEM((T, D), jnp.float32),
            pltpu.VMEM((T, D), jnp.bfloat16),
            pltpu.VMEM(w_in.shape[1:], jnp.bfloat16),
            pltpu.VMEM(w_out.shape[1:], jnp.bfloat16),
            pltpu.VMEM(w_gate_up.shape[1:], jnp.bfloat16),
            pltpu.VMEM(w_down.shape[1:], jnp.bfloat16),
            pltpu.VMEM((CAST_SLOTS, CAST_ROWS, CAST_COLS), jnp.float32),
            pltpu.SemaphoreType.DMA((CAST_SLOTS,)),
            pltpu.SemaphoreType.DMA((1,)),
        ],
        compiler_params=pltpu.CompilerParams(
            dimension_semantics=("arbitrary",),
            vmem_limit_bytes=VMEM_LIMIT_BYTES,
        ),
        name="layer",
    )(*operands)
```

```python
import functools

import jax
import jax.numpy as jnp
from jax.experimental import pallas as pl
from jax.experimental.pallas import tpu as pltpu

EPS = 1e-6
HEAD_DIM = 64
SHORT_K = 3
CFM_K = 31

V7X_SUBLANES = 8
V7X_LANES = 128

TIME_TILE = 64
CONV_TIME = 8
FFN_CHUNK = 256
X_SLOTS = 3
PRE_NORM_PIECES = 8
POST_SPLIT = 2
ADA_COLS = 2048
CAST_ROWS = 256
CAST_COLS = 512
CAST_SLOTS = 10
VMEM_LIMIT_BYTES = 56 * 1024 * 1024


def _unit_rms(v):
    return v * jax.lax.rsqrt(jnp.mean(v * v, axis=-1, keepdims=True) + EPS)


def _sigmoid(v):
    return 1.0 / (1.0 + jnp.exp(-v))


def _tied_rows(v, result):
    bits = jax.lax.bitcast_convert_type(result[:, 0:V7X_LANES], jnp.uint32)
    bits = jax.lax.shift_right_logical(jax.lax.shift_right_logical(bits, jnp.uint32(16)), jnp.uint32(16))
    zero = jax.lax.bitcast_convert_type(bits, jnp.float32)
    return jnp.concatenate([v[:, 0:V7X_LANES] + zero, v[:, V7X_LANES:]], axis=1)


def _adaln_kernel(c_ref, w_ref, b_ref, o_ref):
    c = c_ref[...]
    c_act = (c * _sigmoid(c)).astype(jnp.bfloat16)
    acc = jnp.dot(c_act, w_ref[0].astype(jnp.bfloat16), preferred_element_type=jnp.float32)
    o_ref[...] = acc + b_ref[...]


def _per_batch_affine(v, scale, shift=None):
    nb = scale.shape[0]
    v3 = v.reshape(v.shape[0] // nb, nb, v.shape[1]) * scale[None]
    if shift is not None:
        v3 = v3 + shift[None]
    return v3.reshape(v.shape)


def _fast_dwconv(sh_buf, d1_buf, d2_buf, w_ref, dst_buf, n_taps, nb):
    n_e, n_o = (n_taps + 1) // 2, n_taps // 2
    rows, width = dst_buf.shape
    pairs = CONV_TIME
    blk = pairs * nb

    def corr(buf, r0, c0, taps, weight):
        win = buf[pl.ds(r0, blk + nb * (taps - 1)), c0:c0 + V7X_LANES]
        acc = weight(0) * win[0:blk]
        for m in range(1, taps):
            acc = acc + weight(m) * win[nb * m:nb * m + blk]
        return acc

    def pair_block(i, carry):
        r0 = pl.multiple_of(i * blk, blk)
        for c0 in range(0, width, V7X_LANES):
            lanes = slice(c0, c0 + V7X_LANES)
            shared = sh_buf[pl.ds(r0, blk), lanes]
            even = shared + corr(d1_buf, r0, c0, n_e, lambda m: w_ref[2 * m:2 * m + 1, lanes])
            odd = shared + corr(d2_buf, r0, c0, n_o, lambda m: w_ref[2 * m + 1:2 * m + 2, lanes])
            both = jnp.stack([even.reshape(pairs, nb, V7X_LANES), odd.reshape(pairs, nb, V7X_LANES)], axis=1)
            dst_buf[pl.ds(2 * r0, 2 * blk), lanes] = both.reshape(2 * blk, V7X_LANES)
        return carry

    jax.lax.fori_loop(0, rows // (2 * blk), pair_block, 0)


def _tile_copies(hbm_ref, buf, sem, tile, slot, to_vmem):
    nt, nb = buf.shape[1], buf.shape[2]
    copies = []
    for b in range(nb):
        hbm = hbm_ref.at[b, pl.ds(tile * nt, nt), :]
        vmem = buf.at[slot, :, b, :]
        src, dst = (hbm, vmem) if to_vmem else (vmem, hbm)
        copies.append(pltpu.make_async_copy(src, dst, sem.at[slot]))
    return copies


def _cast_weight(w_hbm, w_vmem, stage, sem):
    rows, cols = w_vmem.shape
    n_c = cols // CAST_COLS
    n = (rows // CAST_ROWS) * n_c
    ahead = CAST_SLOTS - 1

    def chunk_copy(k):
        slot = k % CAST_SLOTS
        r0, c0 = (k // n_c) * CAST_ROWS, (k % n_c) * CAST_COLS
        if not isinstance(k, int):
            r0, c0 = pl.multiple_of(r0, CAST_ROWS), pl.multiple_of(c0, CAST_COLS)
        src = w_hbm.at[0, pl.ds(r0, CAST_ROWS), pl.ds(c0, CAST_COLS)]
        return pltpu.make_async_copy(src, stage.at[slot], sem.at[slot]), slot, r0, c0

    for k in range(min(ahead, n)):
        chunk_copy(k)[0].start()

    def body(k, carry):
        @pl.when(k + ahead < n)
        def _():
            chunk_copy(k + ahead)[0].start()

        cp, slot, r0, c0 = chunk_copy(k)
        cp.wait()
        w_vmem[pl.ds(r0, CAST_ROWS), pl.ds(c0, CAST_COLS)] = stage[slot].astype(jnp.bfloat16)
        return carry

    jax.lax.fori_loop(0, n, body, 0)


def _layer_kernel(x_hbm, mod_ref, g_pre_mix_ref, g_post_mix_ref, w_in_hbm, w_short_ref, b_short_ref,
                  w_cfm_ref, b_cfm_ref, g_ln_ref, b_ln_ref, beta_ref, w_out_hbm,
                  g_pre_ffn_ref, g_post_ffn_ref, w_gu_hbm, w_down_hbm,
                  o_hbm,
                  x_buf, o_buf, in_sem, out_sem, p_buf, ue_buf, uo_buf, d1_buf, d2_buf, sh_buf, w_sum_buf, conv_s, conv_c,
                  f_acc, x1_buf, hb_buf, w_in_ref, w_out_ref, w_gu_ref, w_down_ref, cast_stage, cast_sem,
                  seg_sum_ref, seg_bcast_ref,
                  *, conv_w, ffn_hidden, nb, n_tiles):
    step = pl.program_id(0)
    x_slot = jax.lax.rem(step, X_SLOTS)
    x_next = jax.lax.rem(step + 1, X_SLOTS)
    o_slot = jax.lax.rem(step + 1, 2)
    nt, D = x_buf.shape[1], x_buf.shape[3]
    T = nt * nb
    bf16 = jnp.bfloat16
    f32 = jnp.float32
    p_halo = nb * (SHORT_K - 1)
    s_halo = nb * (CFM_K - 1) // 2
    n_even_taps = (CFM_K + 1) // 2
    w_cfm = w_cfm_ref.at[0]

    mod = mod_ref[...]
    sh1, sc1, gt1, sh2, sc2, gt2 = (mod[:, i * D:(i + 1) * D] for i in range(6))
    gain1 = g_pre_mix_ref[...] * (1.0 + sc1)
    gate1 = g_post_mix_ref[...] * gt1
    gain2 = g_pre_ffn_ref[...] * (1.0 + sc2)
    gate2 = g_post_ffn_ref[...] * gt2

    def pre_norm(slot, rows, beside=None):
        xt = x_buf[slot, rows].reshape(-1, D)
        if beside is not None:
            xt = _tied_rows(xt, beside)
        return _per_batch_affine(_unit_rms(xt), gain1, sh1).astype(bf16)

    def finish_previous(rows):
        r = slice(rows.start * nb, rows.stop * nb)
        out = x1_buf[r, :] + _per_batch_affine(_unit_rms(f_acc[r, :]), gate2)
        o_buf[o_slot, rows] = out.reshape(rows.stop - rows.start, nb, D)

    @pl.when(step == 0)
    def _():
        for t in range(min(2, n_tiles)):
            for cp in _tile_copies(x_hbm, x_buf, in_sem, t, t, True):
                cp.start()
        for w_hbm, w_ref in ((w_in_hbm, w_in_ref), (w_out_hbm, w_out_ref), (w_gu_hbm, w_gu_ref),
                             (w_down_hbm, w_down_ref)):
            _cast_weight(w_hbm, w_ref, cast_stage, cast_sem)
        for cp in _tile_copies(x_hbm, x_buf, in_sem, 0, 0, True):
            cp.wait()
        hb_buf[...] = pre_norm(0, slice(0, nt))
        f_acc[...] = jnp.zeros(f_acc.shape, f32)
        x1_buf[...] = jnp.zeros(x1_buf.shape, f32)
        p_buf[0:p_halo, :] = jnp.zeros((p_halo, conv_w), f32)
        ue_buf[0:s_halo, :] = jnp.zeros((s_halo, conv_w), f32)
        uo_buf[0:s_halo, :] = jnp.zeros((s_halo, conv_w), f32)
        head_shift = HEAD_DIM.bit_length() - 1
        row = jax.lax.broadcasted_iota(jnp.int32, (D, V7X_LANES), 0)
        col = jax.lax.broadcasted_iota(jnp.int32, (D, V7X_LANES), 1)
        seg_sum_ref[...] = jnp.where(jax.lax.shift_right_logical(row, head_shift) == col,
                                     1.0 / HEAD_DIM, 0.0).astype(bf16)
        row = jax.lax.broadcasted_iota(jnp.int32, (V7X_LANES, D), 0)
        col = jax.lax.broadcasted_iota(jnp.int32, (V7X_LANES, D), 1)
        member = jnp.where(row == jax.lax.shift_right_logical(col, head_shift), 1.0, 0.0).astype(bf16)
        seg_bcast_ref[0:V7X_LANES, :] = member
        seg_bcast_ref[V7X_LANES:, :] = member
        for m in range(n_even_taps - 1):
            w_sum_buf[m:m + 1, :] = w_cfm[2 * m:2 * m + 1, :] + w_cfm[2 * m + 1:2 * m + 2, :]
        w_sum_buf[n_even_taps - 1:n_even_taps, :] = w_cfm[CFM_K - 1:CFM_K, :]

    @pl.when(step + 2 < n_tiles)
    def _():
        for cp in _tile_copies(x_hbm, x_buf, in_sem, step + 2, jax.lax.rem(step + 2, X_SLOTS), True):
            cp.start()

    @pl.when(step + 1 < n_tiles)
    def _():
        for cp in _tile_copies(x_hbm, x_buf, in_sem, step + 1, x_next, True):
            cp.wait()

    @pl.when(step >= 3)
    def _():
        for cp in _tile_copies(o_hbm, o_buf, out_sem, step - 3, o_slot, False):
            cp.wait()

    @pl.when(jnp.logical_and(step > 0, step < n_tiles))
    def _():
        p_buf[0:p_halo, :] = p_buf[T:T + p_halo, :]
        ue_buf[0:s_halo, :] = ue_buf[T // 2:T // 2 + s_halo, :]
        uo_buf[0:s_halo, :] = uo_buf[T // 2:T // 2 + s_halo, :]

    @pl.when(step < n_tiles)
    def _():
        hb = hb_buf[...]
        half = nt // 2
        n_pairs = T // 2

        def correlate(buf, taps, weight, init):
            acc = init
            for m in range(taps):
                acc = acc + weight(m) * buf[nb * m:nb * m + n_pairs, :]
            return acc

        finish_previous(slice(0, half))
        z = jnp.dot(hb, w_in_ref[:, 3 * conv_w:5 * conv_w], preferred_element_type=f32)
        finish_previous(slice(half, nt))
        u = (z[:, :conv_w] * _sigmoid(z[:, conv_w:])).reshape(nt // 2, 2, nb, conv_w)
        ue_buf[s_halo:s_halo + n_pairs, :] = u[:, 0].reshape(n_pairs, conv_w)
        uo_buf[s_halo:s_halo + n_pairs, :] = u[:, 1].reshape(n_pairs, conv_w)
        d1_buf[...] = ue_buf[...] - uo_buf[...]
        d2_buf[...] = ue_buf[nb:, :] - uo_buf[0:s_halo + n_pairs - nb, :]

        shared = correlate(uo_buf, n_even_taps, lambda m: w_sum_buf[m:m + 1, :],
                           jnp.broadcast_to(b_cfm_ref[...], (n_pairs, conv_w)))
        z = jnp.dot(hb, w_in_ref[:, 1 * conv_w:3 * conv_w], preferred_element_type=f32)
        p_buf[p_halo:p_halo + T, :] = z[:, :conv_w] * z[:, conv_w:]
        short = jnp.broadcast_to(b_short_ref[...], (T, conv_w))
        for k in range(SHORT_K):
            short = short + w_short_ref[0, k:k + 1, :] * p_buf[nb * k:nb * k + T, :]
        conv_s[...] = short
        gb = jnp.dot(hb, w_in_ref[:, 0:conv_w], preferred_element_type=f32)
        sh_buf[...] = shared
        _fast_dwconv(sh_buf, d1_buf, d2_buf, w_cfm, conv_c, CFM_K, nb)

        groups = [slice(q * T // POST_SPLIT, (q + 1) * T // POST_SPLIT) for q in range(POST_SPLIT)]
        t_groups = [slice(q * nt // POST_SPLIT, (q + 1) * nt // POST_SPLIT) for q in range(POST_SPLIT)]

        def mixed_heads(r):
            uc = conv_c[r, :]
            mu = jnp.mean(uc, axis=-1, keepdims=True)
            d = uc - mu
            var = jnp.mean(d * d, axis=-1, keepdims=True)
            ln = d * jax.lax.rsqrt(var + EPS) * g_ln_ref[...] + b_ln_ref[...]
            return jnp.concatenate([gb[r] * conv_s[r, :], ln * _sigmoid(ln)], axis=-1)

        def head_scale(y):
            ms = jnp.dot((y * y).astype(bf16), seg_sum_ref[...], preferred_element_type=f32)
            r = jax.lax.rsqrt(ms + EPS)
            r_hi = r.astype(bf16)
            r_lo = (r - r_hi.astype(f32)).astype(bf16)
            return jnp.dot(jnp.concatenate([r_hi, r_lo], axis=-1), seg_bcast_ref[...], preferred_element_type=f32)

        def project(y, r_full):
            yn = y * r_full * beta_ref[...]
            return jnp.dot(yn.astype(bf16), w_out_ref[...], preferred_element_type=f32)

        def residual_and_norm(q, o):
            x1 = x_buf[x_slot, t_groups[q]].reshape(-1, D) + _per_batch_affine(_unit_rms(o), gate1)
            x1_buf[groups[q], :] = x1
            return _per_batch_affine(_unit_rms(x1), gain2, sh2).astype(bf16)

        ys, scales, outs, h2_parts = {}, {}, {}, {}
        for stage in range(POST_SPLIT + 3):
            for q in range(POST_SPLIT):
                k = stage - q
                if k == 0:
                    ys[q] = mixed_heads(groups[q])
                elif k == 1:
                    scales[q] = head_scale(ys[q])
                elif k == 2:
                    outs[q] = project(ys[q], scales[q])
                elif k == 3:
                    h2_parts[q] = residual_and_norm(q, outs[q])

        h2 = jnp.concatenate([h2_parts[q] for q in range(POST_SPLIT)], axis=0)
        n_chunks = ffn_hidden // FFN_CHUNK
        pre_rows = nt // PRE_NORM_PIECES
        for j in range(n_chunks):
            c0 = j * FFN_CHUNK
            gate = jnp.dot(h2, w_gu_ref[:, c0:c0 + FFN_CHUNK], preferred_element_type=f32)
            up = jnp.dot(h2, w_gu_ref[:, ffn_hidden + c0:ffn_hidden + c0 + FFN_CHUNK], preferred_element_type=f32)
            act = (gate * _sigmoid(gate) * up).astype(bf16)
            part = jnp.dot(act, w_down_ref[c0:c0 + FFN_CHUNK, :], preferred_element_type=f32)
            if j == 0:
                f_acc[...] = part
            else:
                f_acc[...] += part
            if 1 <= j <= PRE_NORM_PIECES:
                q = j - 1
                r = slice(q * pre_rows * nb, (q + 1) * pre_rows * nb)
                hb_buf[r, :] = pre_norm(x_next, slice(q * pre_rows, (q + 1) * pre_rows), part[r, :])

    @pl.when(step == n_tiles)
    def _():
        finish_previous(slice(0, nt))

    @pl.when(step >= 1)
    def _():
        for cp in _tile_copies(o_hbm, o_buf, out_sem, step - 1, o_slot, False):
            cp.start()

    @pl.when(step == n_tiles)
    def _():
        if n_tiles >= 2:
            for cp in _tile_copies(o_hbm, o_buf, out_sem, step - 2, 1 - o_slot, False):
                cp.wait()
        for cp in _tile_copies(o_hbm, o_buf, out_sem, step - 1, o_slot, False):
            cp.wait()


def _const_spec(shape):
    zeros = (0,) * len(shape)
    return pl.BlockSpec(shape, lambda s: zeros, pipeline_mode=pl.Buffered(1))


def kernel(x, c, w_ada, b_ada, g_pre_mix, g_post_mix, w_in, w_short, b_short, w_cfm_dw, b_cfm_dw,
           g_cfm_ln, b_cfm_ln, beta_mix, w_out, g_pre_ffn, g_post_ffn, w_gate_up, w_down):
    B, S, D = x.shape
    assert w_ada.shape[0] == 1, "single-layer block"
    n_mod = w_ada.shape[2] // D
    conv_w = w_short.shape[2]
    ffn_hidden = w_down.shape[1]
    assert B == V7X_SUBLANES and n_mod == 6
    assert w_in.shape[2] == 5 * conv_w and 2 * conv_w == D
    assert S % TIME_TILE == 0 and TIME_TILE % (2 * CONV_TIME) == 0 and TIME_TILE % (2 * PRE_NORM_PIECES) == 0
    assert CFM_K % 2 == 1
    assert HEAD_DIM & (HEAD_DIM - 1) == 0 and D // HEAD_DIM <= V7X_LANES
    assert ffn_hidden % FFN_CHUNK == 0 and ffn_hidden // FFN_CHUNK > PRE_NORM_PIECES
    assert (n_mod * D) % ADA_COLS == 0
    bf16 = jnp.bfloat16
    T = TIME_TILE * B

    mod = pl.pallas_call(
        _adaln_kernel,
        out_shape=jax.ShapeDtypeStruct((B, n_mod * D), jnp.float32),
        grid=(n_mod * D // ADA_COLS,),
        in_specs=[
            pl.BlockSpec((B, D), lambda j: (0, 0)),
            pl.BlockSpec((1, D, ADA_COLS), lambda j: (0, 0, j)),
            pl.BlockSpec((1, ADA_COLS), lambda j: (0, j)),
        ],
        out_specs=pl.BlockSpec((B, ADA_COLS), lambda j: (0, j)),
        compiler_params=pltpu.CompilerParams(dimension_semantics=("arbitrary",)),
        name="adaln",
    )(c, w_ada, b_ada)

    weights_hbm = (w_in, w_out, w_gate_up, w_down)
    for w in weights_hbm:
        assert w.shape[1] % CAST_ROWS == 0 and w.shape[2] % CAST_COLS == 0
    operands = (
        x, mod, g_pre_mix, g_post_mix, w_in, w_short, b_short,
        w_cfm_dw, b_cfm_dw, g_cfm_ln, b_cfm_ln, beta_mix, w_out,
        g_pre_ffn, g_post_ffn, w_gate_up, w_down,
    )
    in_specs = [pl.BlockSpec(memory_space=pl.ANY) if any(t is h for h in (x,) + weights_hbm)
                else _const_spec(t.shape) for t in operands]
    n_tiles = S // TIME_TILE

    return pl.pallas_call(
        functools.partial(_layer_kernel, conv_w=conv_w, ffn_hidden=ffn_hidden, nb=B, n_tiles=n_tiles),
        out_shape=jax.ShapeDtypeStruct((B, S, D), x.dtype),
        grid=(n_tiles + 1,),
        in_specs=in_specs,
        out_specs=pl.BlockSpec(memory_space=pl.ANY),
        scratch_shapes=[
            pltpu.VMEM((X_SLOTS, TIME_TILE, B, D), jnp.float32),
            pltpu.VMEM((2, TIME_TILE, B, D), jnp.float32),
            pltpu.SemaphoreType.DMA((X_SLOTS,)),
            pltpu.SemaphoreType.DMA((2,)),
            pltpu.VMEM((B * (SHORT_K - 1) + T, conv_w), jnp.float32),
            pltpu.VMEM((B * (CFM_K - 1) // 2 + T // 2, conv_w), jnp.float32),
            pltpu.VMEM((B * (CFM_K - 1) // 2 + T // 2, conv_w), jnp.float32),
            pltpu.VMEM((B * (CFM_K - 1) // 2 + T // 2, conv_w), jnp.float32),
            pltpu.VMEM((B * (CFM_K - 1) // 2 + T // 2 - B, conv_w), jnp.float32),
            pltpu.VMEM((T // 2, conv_w), jnp.float32),
            pltpu.VMEM(((CFM_K + 1) // 2, conv_w), jnp.float32),
            pltpu.VMEM((T, conv_w), jnp.float32),
            pltpu.VMEM((T, conv_w), jnp.float32),
            pltpu.VMEM((T, D), jnp.float32),
            pltpu.VMEM((T, D), jnp.float32),
            pltpu.VMEM((T, D), jnp.bfloat16),
            pltpu.VMEM(w_in.shape[1:], jnp.bfloat16),
            pltpu.VMEM(w_out.shape[1:], jnp.bfloat16),
            pltpu.VMEM(w_gate_up.shape[1:], jnp.bfloat16),
            pltpu.VMEM(w_down.shape[1:], jnp.bfloat16),
            pltpu.VMEM((CAST_SLOTS, CAST_ROWS, CAST_COLS), jnp.float32),
            pltpu.SemaphoreType.DMA((CAST_SLOTS,)),
            pltpu.VMEM((D, V7X_LANES), jnp.bfloat16),
            pltpu.VMEM((2 * V7X_LANES, D), jnp.bfloat16),
        ],
        compiler_params=pltpu.CompilerParams(
            dimension_semantics=("arbitrary",),
            vmem_limit_bytes=VMEM_LIMIT_BYTES,
        ),
        name="layer",
    )(*operands)
```
